```python
import jax, jax.numpy as jnp
from jax import lax
import numpy as np

D_MODEL = 4096
BATCH = 2
SEQ = 8192
DEPTH = 2

N_HEADS = 16
QK_NOPE_DIM = 128
QK_ROPE_DIM = 64
V_HEAD_DIM = 128
Q_LORA_RANK = 896
KV_LORA_RANK = 512
MLA_WIDTH = N_HEADS * V_HEAD_DIM
ATTN_SCALE = (QK_NOPE_DIM + QK_ROPE_DIM) ** -0.5
ROPE_THETA = 10000.0
Q_BLOCK = 128

POOL_WINDOWS = (2, 4, 8, 16)
POOL_GROUPS = len(POOL_WINDOWS)
POOL_WIDTH = D_MODEL - MLA_WIDTH
POOL_GROUP_DIM = POOL_WIDTH // POOL_GROUPS

MIX_WIDTH = MLA_WIDTH + POOL_WIDTH
IN_DIM = Q_LORA_RANK + KV_LORA_RANK + QK_ROPE_DIM + POOL_WIDTH

D_FF = -(-8 * D_MODEL // (3 * 256)) * 256
NORM_EPS = 1e-6

kernel_name = "hybrid_mla_multiscale_pool_sandwich"


def rmsnorm(x, g):
    xf = x.astype(jnp.float32)
    y = xf * lax.rsqrt(jnp.mean(xf * xf, axis=-1, keepdims=True) + NORM_EPS)
    return y.astype(x.dtype) * g


def rope_tables(positions, dtype):
    inv_freq = ROPE_THETA ** (-jnp.arange(0, QK_ROPE_DIM, 2, dtype=jnp.float32) / QK_ROPE_DIM)
    ang = positions.astype(jnp.float32)[..., None] * inv_freq
    return jnp.cos(ang).astype(dtype), jnp.sin(ang).astype(dtype)


def apply_rope(t, cos, sin):
    half = QK_ROPE_DIM // 2
    t1, t2 = t[..., :half], t[..., half:]
    return jnp.concatenate([t1 * cos - t2 * sin, t2 * cos + t1 * sin], axis=-1)


def causal_mla_attention(q_nope, q_rope, k_nope, k_rope, v):
    B, S, H, _ = q_nope.shape
    nb = S // Q_BLOCK
    k_idx = jnp.arange(S)

    def to_blocks(t):
        return t.reshape(B, nb, Q_BLOCK, *t.shape[2:]).swapaxes(0, 1)

    def block(args):
        qn, qr, start = args
        s = (jnp.einsum('bqhd,bkhd->bhqk', qn, k_nope, preferred_element_type=jnp.float32)
             + jnp.einsum('bqhr,bkr->bhqk', qr, k_rope, preferred_element_type=jnp.float32)) * ATTN_SCALE
        q_idx = start + jnp.arange(Q_BLOCK)
        mask = k_idx[None, :] <= q_idx[:, None]
        p = jax.nn.softmax(jnp.where(mask, s, -jnp.inf), axis=-1)
        return jnp.einsum('bhqk,bkhd->bqhd', p.astype(v.dtype), v)

    out = lax.map(block, (to_blocks(q_nope), to_blocks(q_rope), jnp.arange(nb) * Q_BLOCK))
    return out.swapaxes(0, 1).reshape(B, S, H, V_HEAD_DIM)


def multiscale_pool(u, pool_w, pool_scale):
    B, S, _ = u.shape
    groups = u.astype(jnp.float32).reshape(B, S, POOL_GROUPS, POOL_GROUP_DIM)
    cs = jnp.cumsum(groups, axis=1)
    t = jnp.arange(S)
    diffs = []
    for g, w in enumerate(POOL_WINDOWS):
        c = cs[:, :, g]
        lag = jnp.pad(c, ((0, 0), (w, 0), (0, 0)))[:, :S]
        cnt = jnp.minimum(t + 1, w).astype(jnp.float32)[None, :, None]
        diffs.append((c - lag) / cnt - groups[:, :, g])
    d = jnp.stack(diffs, axis=2).astype(u.dtype)
    y = jnp.einsum('bsgc,gcd->bsgd', d, pool_w).reshape(B, S, POOL_WIDTH)
    return y * pool_scale


def hybrid_mixer(a, cos, sin, w_in, q_norm, w_q_up, kv_norm, w_kv_up, pool_w, pool_scale, w_out):
    B, S, _ = a.shape
    z = a @ w_in
    o1 = Q_LORA_RANK
    o2 = o1 + KV_LORA_RANK
    o3 = o2 + QK_ROPE_DIM
    c_q, c_kv, k_rope, u = z[..., :o1], z[..., o1:o2], z[..., o2:o3], z[..., o3:]
    q = (rmsnorm(c_q, q_norm) @ w_q_up).reshape(B, S, N_HEADS, QK_NOPE_DIM + QK_ROPE_DIM)
    q_nope = q[..., :QK_NOPE_DIM]
    q_rope = apply_rope(q[..., QK_NOPE_DIM:], cos[:, :, None, :], sin[:, :, None, :])
    k_rope = apply_rope(k_rope, cos, sin)
    kv = (rmsnorm(c_kv, kv_norm) @ w_kv_up).reshape(B, S, N_HEADS, QK_NOPE_DIM + V_HEAD_DIM)
    k_nope, v = kv[..., :QK_NOPE_DIM], kv[..., QK_NOPE_DIM:]
    attn = causal_mla_attention(q_nope, q_rope, k_nope, k_rope, v).reshape(B, S, MLA_WIDTH)
    pool = multiscale_pool(u, pool_w, pool_scale)
    return jnp.concatenate([attn, pool], axis=-1) @ w_out


def swiglu(h, w_gate, w_up, w_down):
    return (jax.nn.silu(h @ w_gate) * (h @ w_up)) @ w_down


def setup_inputs(seed: int = 0) -> dict:
    key = jax.random.key(seed)
    ks = jax.random.split(key, 20)

    def w(k, shape, fan_in):
        return jax.random.normal(k, shape, jnp.float32) * fan_in ** -0.5

    def gain(k, shape):
        return 1.0 + 0.02 * jax.random.normal(k, shape, jnp.float32)

    x = jax.random.normal(ks[0], (BATCH, SEQ, D_MODEL), jnp.float32)
    offsets = jax.random.randint(ks[1], (BATCH, 1), 0, 1024, dtype=jnp.int32)
    positions = offsets + jnp.arange(SEQ, dtype=jnp.int32)[None, :]
    return {
        "x": x,
        "positions": positions,
        "w_in": w(ks[2], (DEPTH, D_MODEL, IN_DIM), D_MODEL),
        "q_norm": gain(ks[3], (DEPTH, Q_LORA_RANK)),
        "w_q_up": w(ks[4], (DEPTH, Q_LORA_RANK, N_HEADS * (QK_NOPE_DIM + QK_ROPE_DIM)), Q_LORA_RANK),
        "kv_norm": gain(ks[5], (DEPTH, KV_LORA_RANK)),
        "w_kv_up": w(ks[6], (DEPTH, KV_LORA_RANK, N_HEADS * (QK_NOPE_DIM + V_HEAD_DIM)), KV_LORA_RANK),
        "pool_w": w(ks[7], (DEPTH, POOL_GROUPS, POOL_GROUP_DIM, POOL_GROUP_DIM), POOL_GROUP_DIM),
        "pool_scale": gain(ks[8], (DEPTH, POOL_WIDTH)),
        "w_out": w(ks[9], (DEPTH, MIX_WIDTH, D_MODEL), MIX_WIDTH),
        "norm_pre_mix": gain(ks[10], (DEPTH, D_MODEL)),
        "norm_post_mix": gain(ks[11], (DEPTH, D_MODEL)),
        "norm_pre_ffn": gain(ks[12], (DEPTH, D_MODEL)),
        "norm_post_ffn": gain(ks[13], (DEPTH, D_MODEL)),
        "w_gate": w(ks[14], (DEPTH, D_MODEL, D_FF), D_MODEL),
        "w_up": w(ks[15], (DEPTH, D_MODEL, D_FF), D_MODEL),
        "w_down": w(ks[16], (DEPTH, D_FF, D_MODEL), D_FF),
    }


def reference(x, positions, w_in, q_norm, w_q_up, kv_norm, w_kv_up, pool_w, pool_scale, w_out,
              norm_pre_mix, norm_post_mix, norm_pre_ffn, norm_post_ffn, w_gate, w_up, w_down):
    cos, sin = rope_tables(positions, x.dtype)
    h = x
    for l in range(DEPTH):
        a = rmsnorm(h, norm_pre_mix[l])
        m = hybrid_mixer(a, cos, sin, w_in[l], q_norm[l], w_q_up[l], kv_norm[l], w_kv_up[l],
                         pool_w[l], pool_scale[l], w_out[l])
        h = h + rmsnorm(m, norm_post_mix[l])
        f = swiglu(rmsnorm(h, norm_pre_ffn[l]), w_gate[l], w_up[l], w_down[l])
        h = h + rmsnorm(f, norm_post_ffn[l])
    return h
```

```python
import functools

import jax
import jax.numpy as jnp
from jax import lax
from jax.experimental import pallas as pl
from jax.experimental.pallas import tpu as pltpu

F32 = jnp.float32
BF16 = jnp.bfloat16

N_HEADS = 16
QK_NOPE_DIM = 128
QK_ROPE_DIM = 64
V_HEAD_DIM = 128
Q_LORA_RANK = 896
KV_LORA_RANK = 512
POOL_WINDOWS = (2, 4, 8, 16)
POOL_GROUP_DIM = 512
POOL_WIDTH = POOL_GROUP_DIM * len(POOL_WINDOWS)
MLA_WIDTH = N_HEADS * V_HEAD_DIM
ROPE_PAIR_WIDTH = 2 * QK_ROPE_DIM
ATTN_SCALE = (QK_NOPE_DIM + QK_ROPE_DIM) ** -0.5
ROPE_THETA = 10000.0
NORM_EPS = 1e-6
MASK_VALUE = -1e30

LANES = 128
POOL_HALO = 16
VMEM_LIMIT_BYTES = 56 * 1024 * 1024

Z_U = 0
Z_KV = POOL_WIDTH
Z_KR = Z_KV + KV_LORA_RANK
Z_Q = Z_KR + 2 * QK_ROPE_DIM
Z_WIDTH = Z_Q + Q_LORA_RANK


def _params(*sem):
    return pltpu.CompilerParams(dimension_semantics=sem, vmem_limit_bytes=VMEM_LIMIT_BYTES)


def _rms(x, g):
    ms = jnp.mean(x * x, axis=-1, keepdims=True)
    return x * lax.rsqrt(ms + NORM_EPS) * g


def _rope_table_kernel(pos_ref, invf_ref, sign_ref, cos_ref, sin_ref):
    ang = pos_ref[...].astype(F32) * invf_ref[...]
    cos_ref[...] = jnp.cos(ang)
    sin_ref[...] = jnp.sin(ang) * sign_ref[...]


def rope_tables(pos_col, invf, sign, rows):
    n = pos_col.shape[0]
    return pl.pallas_call(
        _rope_table_kernel,
        grid=(n // rows,),
        in_specs=[pl.BlockSpec((rows, 1), lambda i: (i, 0)),
                  pl.BlockSpec((1, LANES), lambda i: (0, 0)),
                  pl.BlockSpec((1, LANES), lambda i: (0, 0))],
        out_specs=[pl.BlockSpec((rows, LANES), lambda i: (i, 0))] * 2,
        out_shape=[jax.ShapeDtypeStruct((n, LANES), F32)] * 2,
        compiler_params=_params("parallel"),
        name="rope_tables",
    )(pos_col, invf, sign)


def _norm_kernel(x_ref, g_ref, o_ref):
    o_ref[...] = _rms(x_ref[...], g_ref[...]).astype(BF16)


def norm_cast(x, g, rows):
    n, d = x.shape
    return pl.pallas_call(
        _norm_kernel,
        grid=(n // rows,),
        in_specs=[pl.BlockSpec((rows, d), lambda i: (i, 0)),
                  pl.BlockSpec((1, d), lambda i: (0, 0))],
        out_specs=pl.BlockSpec((rows, d), lambda i: (i, 0)),
        out_shape=jax.ShapeDtypeStruct((n, d), BF16),
        compiler_params=_params("parallel"),
        name="norm_cast",
    )(x, g)


def _mm_kernel(a_ref, w_ref, o_ref):
    o_ref[...] = jnp.dot(a_ref[...], w_ref[...], preferred_element_type=F32).astype(o_ref.dtype)


def matmul(a, w, out_dtype, tm, tn):
    m, k = a.shape
    n = w.shape[1]
    return pl.pallas_call(
        _mm_kernel,
        grid=(m // tm, n // tn),
        in_specs=[pl.BlockSpec((tm, k), lambda i, j: (i, 0)),
                  pl.BlockSpec((k, tn), lambda i, j: (0, j))],
        out_specs=pl.BlockSpec((tm, tn), lambda i, j: (i, j)),
        out_shape=jax.ShapeDtypeStruct((m, n), out_dtype),
        compiler_params=_params("parallel", "arbitrary"),
        name="matmul",
    )(a, w)


def _mm2_kernel(a1_ref, a2_ref, w1_ref, w2_ref, o_ref):
    acc = jnp.dot(a1_ref[...], w1_ref[...], preferred_element_type=F32)
    acc += jnp.dot(a2_ref[...], w2_ref[...], preferred_element_type=F32)
    o_ref[...] = acc


def matmul_concat2(a1, a2, w, tm, tn):
    m, k1 = a1.shape
    k2 = a2.shape[1]
    assert k1 == k2 and w.shape[0] == k1 + k2
    n = w.shape[1]
    return pl.pallas_call(
        _mm2_kernel,
        grid=(m // tm, n // tn),
        in_specs=[pl.BlockSpec((tm, k1), lambda i, j: (i, 0)),
                  pl.BlockSpec((tm, k2), lambda i, j: (i, 0)),
                  pl.BlockSpec((k1, tn), lambda i, j: (0, j)),
                  pl.BlockSpec((k2, tn), lambda i, j: (1, j))],
        out_specs=pl.BlockSpec((tm, tn), lambda i, j: (i, j)),
        out_shape=jax.ShapeDtypeStruct((m, n), F32),
        compiler_params=_params("parallel", "arbitrary"),
        name="matmul_concat2",
    )(a1, a2, w, w)


def _qkv_kernel(cq_ref, ckv_ref, kr_ref, cos_ref, sin_ref, qg_ref, kvg_ref, wq_ref, wkv_ref,
                qn_ref, qr_ref, kn_ref, v_ref, kro_ref):
    cos = cos_ref[...]
    sin = sin_ref[...]
    tm = cos.shape[0]
    lane = lax.broadcasted_iota(jnp.int32, (tm, LANES), 1)
    first_half = (lane % QK_ROPE_DIM) < (QK_ROPE_DIM // 2)
    chunk = 512

    cqn = _rms(cq_ref[...], qg_ref[...]).astype(BF16)
    for c in range(MLA_WIDTH // chunk):
        sl = slice(c * chunk, (c + 1) * chunk)
        qn_ref[:, sl] = jnp.dot(cqn, wq_ref[:, sl], preferred_element_type=F32).astype(BF16)
    t_all = jnp.dot(cqn, wq_ref[:, MLA_WIDTH:], preferred_element_type=F32)
    for c in range(t_all.shape[1] // LANES):
        t = t_all[:, c * LANES:(c + 1) * LANES]
        swapped = jnp.where(first_half, pltpu.roll(t, LANES - 32, 1), pltpu.roll(t, 32, 1))
        qr_ref[:, c * LANES:(c + 1) * LANES] = (t * cos + swapped * sin).astype(BF16)

    ckvn = _rms(ckv_ref[...], kvg_ref[...]).astype(BF16)
    for c in range(MLA_WIDTH // chunk):
        sl = slice(c * chunk, (c + 1) * chunk)
        sl_v = slice(MLA_WIDTH + c * chunk, MLA_WIDTH + (c + 1) * chunk)
        kn_ref[:, sl] = jnp.dot(ckvn, wkv_ref[:, sl], preferred_element_type=F32).astype(BF16)
        v_ref[:, sl] = jnp.dot(ckvn, wkv_ref[:, sl_v], preferred_element_type=F32).astype(BF16)

    kr = kr_ref[...]
    kr_rot = kr * cos + pltpu.roll(kr, 32, 1) * sin
    zero = jnp.zeros_like(kr_rot)
    kro_ref[:, :LANES] = jnp.where(lane < QK_ROPE_DIM, kr_rot, zero).astype(BF16)
    kro_ref[:, LANES:] = jnp.where(lane >= QK_ROPE_DIM, kr_rot, zero).astype(BF16)


def qkv_project(z, cos, sin, q_gain, kv_gain, wq, wkv, tm):
    n = z.shape[0]
    rope_w = N_HEADS * QK_ROPE_DIM
    row = lambda i: (i, 0)
    const = lambda i: (0, 0)
    return pl.pallas_call(
        _qkv_kernel,
        grid=(n // tm,),
        in_specs=[pl.BlockSpec((tm, Q_LORA_RANK), lambda i: (i, Z_Q // Q_LORA_RANK)),
                  pl.BlockSpec((tm, KV_LORA_RANK), lambda i: (i, Z_KV // KV_LORA_RANK)),
                  pl.BlockSpec((tm, LANES), lambda i: (i, Z_KR // LANES)),
                  pl.BlockSpec((tm, LANES), row),
                  pl.BlockSpec((tm, LANES), row),
                  pl.BlockSpec((1, Q_LORA_RANK), const),
                  pl.BlockSpec((1, KV_LORA_RANK), const),
                  pl.BlockSpec(wq.shape, const),
                  pl.BlockSpec(wkv.shape, const)],
        out_specs=[pl.BlockSpec((tm, MLA_WIDTH), row),
                   pl.BlockSpec((tm, rope_w), row),
                   pl.BlockSpec((tm, MLA_WIDTH), row),
                   pl.BlockSpec((tm, MLA_WIDTH), row),
                   pl.BlockSpec((tm, 2 * LANES), row)],
        out_shape=[jax.ShapeDtypeStruct((n, MLA_WIDTH), BF16),
                   jax.ShapeDtypeStruct((n, rope_w), BF16),
                   jax.ShapeDtypeStruct((n, MLA_WIDTH), BF16),
                   jax.ShapeDtypeStruct((n, MLA_WIDTH), BF16),
                   jax.ShapeDtypeStruct((n, 2 * LANES), BF16)],
        compiler_params=_params("parallel"),
        name="qkv_project",
    )(z, z, z, cos, sin, q_gain, kv_gain, wq, wkv)


def _attn_kernel(qn_ref, qr_ref, kn_ref, kr_ref, v_ref, o_ref, m_scr, l_scr, acc_scr, *, tq, tk):
    i = pl.program_id(2)
    q = jnp.concatenate([qn_ref[...], qr_ref[...]], axis=1)
    m_scr[...] = jnp.full(m_scr.shape, MASK_VALUE, F32)
    l_scr[...] = jnp.zeros(l_scr.shape, F32)
    acc_scr[...] = jnp.zeros(acc_scr.shape, F32)

    def step(start, diag_offset):
        k = jnp.concatenate([kn_ref[pl.ds(start, tk), :], kr_ref[pl.ds(start, tk), :]], axis=1)
        s = lax.dot_general(q, k, (((1,), (1,)), ((), ())), preferred_element_type=F32) * ATTN_SCALE
        if diag_offset is not None:
            row = lax.broadcasted_iota(jnp.int32, (tq, tk), 0)
            col = lax.broadcasted_iota(jnp.int32, (tq, tk), 1) + diag_offset
            s = jnp.where(col <= row, s, MASK_VALUE)
        m_prev = m_scr[...]
        m_next = jnp.maximum(m_prev, jnp.max(s, axis=1, keepdims=True))
        alpha = jnp.exp(m_prev - m_next)
        ps = [jnp.exp(s[:, c * LANES:(c + 1) * LANES] - m_next) for c in range(tk // LANES)]
        l_scr[...] = alpha * l_scr[...] + functools.reduce(lambda a, b: a + b, ps)
        p = jnp.concatenate(ps, axis=1).astype(BF16)
        acc_scr[...] = alpha * acc_scr[...] + jnp.dot(p, v_ref[pl.ds(start, tk), :],
                                                      preferred_element_type=F32)
        m_scr[...] = m_next

    def body(j, carry):
        step(pl.multiple_of(j * tk, tk), None)
        return carry

    lax.fori_loop(0, i * (tq // tk), body, 0)
    for d in range(tq // tk):
        step(pl.multiple_of(i * tq + d * tk, tk), d * tk)

    l = jnp.sum(l_scr[...], axis=1, keepdims=True)
    o_ref[...] = (acc_scr[...] / l).astype(o_ref.dtype)


def causal_attention(qn, qr, kn, kr, v, batch, seq, tq, tk):
    n = qn.shape[0]
    nq = seq // tq
    return pl.pallas_call(
        functools.partial(_attn_kernel, tq=tq, tk=tk),
        grid=(batch, N_HEADS, nq),
        in_specs=[pl.BlockSpec((tq, QK_NOPE_DIM), lambda b, h, i: (b * nq + i, h)),
                  pl.BlockSpec((tq, ROPE_PAIR_WIDTH), lambda b, h, i: (b * nq + i, h // 2)),
                  pl.BlockSpec((seq, QK_NOPE_DIM), lambda b, h, i: (b, h)),
                  pl.BlockSpec((seq, ROPE_PAIR_WIDTH), lambda b, h, i: (b, h % 2)),
                  pl.BlockSpec((seq, V_HEAD_DIM), lambda b, h, i: (b, h))],
        out_specs=pl.BlockSpec((tq, V_HEAD_DIM), lambda b, h, i: (b * nq + i, h)),
        out_shape=jax.ShapeDtypeStruct((n, MLA_WIDTH), BF16),
        scratch_shapes=[pltpu.VMEM((tq, LANES), F32),
                        pltpu.VMEM((tq, LANES), F32),
                        pltpu.VMEM((tq, V_HEAD_DIM), F32)],
        compiler_params=_params("parallel", "parallel", "arbitrary"),
        name="causal_attention",
    )(qn, qr, kn, kr, v)


def _pool_kernel(u_ref, halo_ref, w_ref, scale_ref, o_ref, *, ts):
    i = pl.program_id(1)
    t = i * ts + lax.broadcasted_iota(jnp.int32, (ts, 1), 0)
    for g, window in enumerate(POOL_WINDOWS):
        sl = slice(g * POOL_GROUP_DIM, (g + 1) * POOL_GROUP_DIM)
        x = u_ref[:, sl]
        halo = jnp.where(i > 0, halo_ref[:, sl], 0.0)
        s = jnp.concatenate([halo, x], axis=0)
        shift = 1
        while shift < window:
            s = s + pltpu.roll(s, shift, 0)
            shift *= 2
        cnt = jnp.minimum(t + 1, window).astype(F32)
        d = s[POOL_HALO:, :] / cnt - x
        y = jnp.dot(d.astype(BF16), w_ref[g], preferred_element_type=F32)
        o_ref[:, sl] = (y * scale_ref[:, sl]).astype(o_ref.dtype)


def multiscale_pool(z, pool_w, pool_scale, batch, seq, ts):
    n = z.shape[0]
    ns = seq // ts
    halo_blocks = ts // POOL_HALO
    return pl.pallas_call(
        functools.partial(_pool_kernel, ts=ts),
        grid=(batch, ns),
        in_specs=[pl.BlockSpec((ts, POOL_WIDTH), lambda b, i: (b * ns + i, Z_U // POOL_WIDTH)),
                  pl.BlockSpec((POOL_HALO, POOL_WIDTH),
                               lambda b, i: (jnp.maximum((b * ns + i) * halo_blocks - 1, 0), 0)),
                  pl.BlockSpec(pool_w.shape, lambda b, i: (0, 0, 0)),
                  pl.BlockSpec((1, POOL_WIDTH), lambda b, i: (0, 0))],
        out_specs=pl.BlockSpec((ts, POOL_WIDTH), lambda b, i: (b * ns + i, 0)),
        out_shape=jax.ShapeDtypeStruct((n, POOL_WIDTH), BF16),
        compiler_params=_params("parallel", "parallel"),
        name="multiscale_pool",
    )(z, z, pool_w, pool_scale)


def _resid_norm_kernel(m_ref, h_ref, gpost_ref, gnext_ref, ho_ref, ao_ref):
    h = h_ref[...] + _rms(m_ref[...], gpost_ref[...])
    ho_ref[...] = h
    ao_ref[...] = _rms(h, gnext_ref[...]).astype(BF16)


def _resid_kernel(m_ref, h_ref, gpost_ref, ho_ref):
    ho_ref[...] = h_ref[...] + _rms(m_ref[...], gpost_ref[...])


def resid_norm(m, h, g_post, g_next, rows):
    n, d = h.shape
    row = pl.BlockSpec((rows, d), lambda i: (i, 0))
    gain = pl.BlockSpec((1, d), lambda i: (0, 0))
    if g_next is None:
        return pl.pallas_call(
            _resid_kernel, grid=(n // rows,),
            in_specs=[row, row, gain], out_specs=row,
            out_shape=jax.ShapeDtypeStruct((n, d), F32),
            compiler_params=_params("parallel"), name="resid",
        )(m, h, g_post), None
    return pl.pallas_call(
        _resid_norm_kernel, grid=(n // rows,),
        in_specs=[row, row, gain, gain], out_specs=[row, row],
        out_shape=[jax.ShapeDtypeStruct((n, d), F32), jax.ShapeDtypeStruct((n, d), BF16)],
        compiler_params=_params("parallel"), name="resid_norm",
    )(m, h, g_post, g_next)


def _ffn_kernel(a_ref, wg_ref, wu_ref, wd_ref, o_ref, *, chunk):
    f = pl.program_id(1)
    a = a_ref[...]
    g = jnp.dot(a, wg_ref[...], preferred_element_type=F32)
    u = jnp.dot(a, wu_ref[...], preferred_element_type=F32)
    p = (g * (1.0 / (1.0 + jnp.exp(-g))) * u).astype(BF16)
    d = o_ref.shape[1]

    @pl.when(f == 0)
    def _():
        for c in range(d // chunk):
            sl = slice(c * chunk, (c + 1) * chunk)
            o_ref[:, sl] = jnp.dot(p, wd_ref[:, sl], preferred_element_type=F32)

    @pl.when(f > 0)
    def _():
        for c in range(d // chunk):
            sl = slice(c * chunk, (c + 1) * chunk)
            o_ref[:, sl] += jnp.dot(p, wd_ref[:, sl], preferred_element_type=F32)


def swiglu(a, w_gate, w_up, w_down, tm, tf):
    n, d = a.shape
    d_ff = w_gate.shape[1]
    return pl.pallas_call(
        functools.partial(_ffn_kernel, chunk=512),
        grid=(n // tm, d_ff // tf),
        in_specs=[pl.BlockSpec((tm, d), lambda i, f: (i, 0)),
                  pl.BlockSpec((d, tf), lambda i, f: (0, f)),
                  pl.BlockSpec((d, tf), lambda i, f: (0, f)),
                  pl.BlockSpec((tf, d), lambda i, f: (f, 0))],
        out_specs=pl.BlockSpec((tm, d), lambda i, f: (i, 0)),
        out_shape=jax.ShapeDtypeStruct((n, d), F32),
        compiler_params=_params("parallel", "arbitrary"),
        name="swiglu",
    )(a, w_gate, w_up, w_down)


def _prep_w_in(w_in):
    o1 = Q_LORA_RANK
    o2 = o1 + KV_LORA_RANK
    o3 = o2 + QK_ROPE_DIM
    kr = w_in[..., o2:o3]
    return jnp.concatenate([w_in[..., o3:], w_in[..., o1:o2], kr, kr, w_in[..., :o1]], axis=-1).astype(BF16)


def _prep_w_q(w_q_up):
    depth = w_q_up.shape[0]
    w = w_q_up.reshape(depth, Q_LORA_RANK, N_HEADS, QK_NOPE_DIM + QK_ROPE_DIM)
    nope = w[..., :QK_NOPE_DIM].reshape(depth, Q_LORA_RANK, N_HEADS * QK_NOPE_DIM)
    rope = w[..., QK_NOPE_DIM:].reshape(depth, Q_LORA_RANK, N_HEADS * QK_ROPE_DIM)
    return jnp.concatenate([nope, rope], axis=-1).astype(BF16)


def _prep_w_kv(w_kv_up):
    depth = w_kv_up.shape[0]
    w = w_kv_up.reshape(depth, KV_LORA_RANK, N_HEADS, QK_NOPE_DIM + V_HEAD_DIM)
    k_nope = w[..., :QK_NOPE_DIM].reshape(depth, KV_LORA_RANK, N_HEADS * QK_NOPE_DIM)
    v = w[..., QK_NOPE_DIM:].reshape(depth, KV_LORA_RANK, N_HEADS * V_HEAD_DIM)
    return jnp.concatenate([k_nope, v], axis=-1).astype(BF16)


def kernel(x, positions, w_in, q_norm, w_q_up, kv_norm, w_kv_up, pool_w, pool_scale, w_out,
           norm_pre_mix, norm_post_mix, norm_pre_ffn, norm_post_ffn, w_gate, w_up, w_down):
    batch, seq, d_model = x.shape
    depth = w_in.shape[0]
    n = batch * seq

    tm_big = min(1024, n)
    tm_mid = min(512, n)
    rows_norm = min(256, n)
    tq = tk = min(512, seq)
    ts = min(512, seq)

    w_in_p = _prep_w_in(w_in)
    w_q_p = _prep_w_q(w_q_up)
    w_kv_p = _prep_w_kv(w_kv_up)
    pool_w_b = pool_w.astype(BF16)
    w_out_b = w_out.astype(BF16)
    w_gate_b = w_gate.astype(BF16)
    w_up_b = w_up.astype(BF16)
    w_down_b = w_down.astype(BF16)

    inv_freq = ROPE_THETA ** (-jnp.arange(0, QK_ROPE_DIM, 2, dtype=F32) / QK_ROPE_DIM)
    invf = jnp.tile(inv_freq, LANES // inv_freq.shape[0])[None, :]
    half = QK_ROPE_DIM // 2
    sign = jnp.where((jnp.arange(LANES) % QK_ROPE_DIM) < half, -1.0, 1.0).astype(F32)[None, :]
    cos, sin = rope_tables(positions.reshape(n, 1), invf, sign, min(2048, n))

    gain = lambda g, l: g[l][None, :]
    h = x.reshape(n, d_model)
    a = norm_cast(h, gain(norm_pre_mix, 0), rows_norm)
    for l in range(depth):
        z = matmul(a, w_in_p[l], F32, tm_big, 512)
        qn, qr, kn, v, kr = qkv_project(z, cos, sin, gain(q_norm, l), gain(kv_norm, l),
                                        w_q_p[l], w_kv_p[l], tm_mid)
        attn = causal_attention(qn, qr, kn, kr, v, batch, seq, tq, tk)
        pool = multiscale_pool(z, pool_w_b[l], gain(pool_scale, l), batch, seq, ts)
        m = matmul_concat2(attn, pool, w_out_b[l], tm_big, 512)
        h, a = resid_norm(m, h, gain(norm_post_mix, l), gain(norm_pre_ffn, l), rows_norm)
        f = swiglu(a, w_gate_b[l], w_up_b[l], w_down_b[l], tm_mid, 256)
        g_next = gain(norm_pre_mix, l + 1) if l + 1 < depth else None
        h, a = resid_norm(f, h, gain(norm_post_ffn, l), g_next, rows_norm)
    return h.reshape(batch, seq, d_model)
```

```python
import functools
import math

import jax
import jax.numpy as jnp
from jax import lax
from jax.experimental import pallas as pl
from jax.experimental.pallas import tpu as pltpu

F32 = jnp.float32
BF16 = jnp.bfloat16

N_HEADS = 16
QK_NOPE_DIM = 128
QK_ROPE_DIM = 64
V_HEAD_DIM = 128
Q_LORA_RANK = 896
KV_LORA_RANK = 512
POOL_WINDOWS = (2, 4, 8, 16)
POOL_GROUP_DIM = 512
POOL_WIDTH = POOL_GROUP_DIM * len(POOL_WINDOWS)
MLA_WIDTH = N_HEADS * V_HEAD_DIM
ROPE_PAIR_WIDTH = 2 * QK_ROPE_DIM
ATTN_SCALE = (QK_NOPE_DIM + QK_ROPE_DIM) ** -0.5
Q_PRESCALE = ATTN_SCALE * math.log2(math.e)
ROPE_THETA = 10000.0
NORM_EPS = 1e-6
MASK_VALUE = -1e30

LANES = 128
POOL_HALO = 16
VMEM_LIMIT_BYTES = 56 * 1024 * 1024

Z_U = 0
Z_KV = POOL_WIDTH
Z_KR = Z_KV + KV_LORA_RANK
Z_Q = Z_KR + 2 * QK_ROPE_DIM
Z_WIDTH = Z_Q + Q_LORA_RANK


def _params(*sem):
    return pltpu.CompilerParams(dimension_semantics=sem, vmem_limit_bytes=VMEM_LIMIT_BYTES)


def _rms(x, g):
    ms = jnp.mean(x * x, axis=-1, keepdims=True)
    return x * lax.rsqrt(ms + NORM_EPS) * g


def _rope_table_kernel(pos_ref, invf_ref, sign_ref, cos_ref, sin_ref):
    ang = pos_ref[...].astype(F32) * invf_ref[...]
    cos_ref[...] = jnp.cos(ang)
    sin_ref[...] = jnp.sin(ang) * sign_ref[...]


def rope_tables(pos_col, invf, sign, rows):
    n = pos_col.shape[0]
    return pl.pallas_call(
        _rope_table_kernel,
        grid=(n // rows,),
        in_specs=[pl.BlockSpec((rows, 1), lambda i: (i, 0)),
                  pl.BlockSpec((1, LANES), lambda i: (0, 0)),
                  pl.BlockSpec((1, LANES), lambda i: (0, 0))],
        out_specs=[pl.BlockSpec((rows, LANES), lambda i: (i, 0))] * 2,
        out_shape=[jax.ShapeDtypeStruct((n, LANES), F32)] * 2,
        compiler_params=_params("parallel"),
        name="rope_tables",
    )(pos_col, invf, sign)


def _norm_kernel(x_ref, g_ref, o_ref):
    o_ref[...] = _rms(x_ref[...], g_ref[...]).astype(BF16)


def norm_cast(x, g, rows):
    n, d = x.shape
    return pl.pallas_call(
        _norm_kernel,
        grid=(n // rows,),
        in_specs=[pl.BlockSpec((rows, d), lambda i: (i, 0)),
                  pl.BlockSpec((1, d), lambda i: (0, 0))],
        out_specs=pl.BlockSpec((rows, d), lambda i: (i, 0)),
        out_shape=jax.ShapeDtypeStruct((n, d), BF16),
        compiler_params=_params("parallel"),
        name="norm_cast",
    )(x, g)


def _mm_kernel(a_ref, w_ref, o_ref):
    o_ref[...] = jnp.dot(a_ref[...], w_ref[...], preferred_element_type=F32).astype(o_ref.dtype)


def matmul(a, w, out_dtype, tm, tn):
    m, k = a.shape
    n = w.shape[1]
    return pl.pallas_call(
        _mm_kernel,
        grid=(m // tm, n // tn),
        in_specs=[pl.BlockSpec((tm, k), lambda i, j: (i, 0)),
                  pl.BlockSpec((k, tn), lambda i, j: (0, j))],
        out_specs=pl.BlockSpec((tm, tn), lambda i, j: (i, j)),
        out_shape=jax.ShapeDtypeStruct((m, n), out_dtype),
        compiler_params=_params("parallel", "arbitrary"),
        name="matmul",
    )(a, w)


def _mm2_kernel(a1_ref, a2_ref, w1_ref, w2_ref, o_ref):
    acc = jnp.dot(a1_ref[...], w1_ref[...], preferred_element_type=F32)
    acc += jnp.dot(a2_ref[...], w2_ref[...], preferred_element_type=F32)
    o_ref[...] = acc


def matmul_concat2(a1, a2, w, tm, tn):
    m, k1 = a1.shape
    k2 = a2.shape[1]
    assert k1 == k2 and w.shape[0] == k1 + k2
    n = w.shape[1]
    return pl.pallas_call(
        _mm2_kernel,
        grid=(m // tm, n // tn),
        in_specs=[pl.BlockSpec((tm, k1), lambda i, j: (i, 0)),
                  pl.BlockSpec((tm, k2), lambda i, j: (i, 0)),
                  pl.BlockSpec((k1, tn), lambda i, j: (0, j)),
                  pl.BlockSpec((k2, tn), lambda i, j: (1, j))],
        out_specs=pl.BlockSpec((tm, tn), lambda i, j: (i, j)),
        out_shape=jax.ShapeDtypeStruct((m, n), F32),
        compiler_params=_params("parallel", "arbitrary"),
        name="matmul_concat2",
    )(a1, a2, w, w)


def _qkv_kernel(cq_ref, ckv_ref, kr_ref, cos_ref, sin_ref, qg_ref, kvg_ref, wq_ref, wkv_ref,
                qn_ref, qr_ref, kn_ref, v_ref, kro_ref):
    cos = cos_ref[...]
    sin = sin_ref[...]
    tm = cos.shape[0]
    lane = lax.broadcasted_iota(jnp.int32, (tm, LANES), 1)
    first_half = (lane % QK_ROPE_DIM) < (QK_ROPE_DIM // 2)
    chunk = 512

    cqn = _rms(cq_ref[...], qg_ref[...]).astype(BF16)
    for c in range(MLA_WIDTH // chunk):
        sl = slice(c * chunk, (c + 1) * chunk)
        qn = jnp.dot(cqn, wq_ref[:, sl], preferred_element_type=F32)
        qn_ref[:, sl] = (qn * Q_PRESCALE).astype(BF16)
    t_all = jnp.dot(cqn, wq_ref[:, MLA_WIDTH:], preferred_element_type=F32)
    for c in range(t_all.shape[1] // LANES):
        t = t_all[:, c * LANES:(c + 1) * LANES]
        swapped = jnp.where(first_half, pltpu.roll(t, LANES - 32, 1), pltpu.roll(t, 32, 1))
        qr_ref[:, c * LANES:(c + 1) * LANES] = ((t * cos + swapped * sin) * Q_PRESCALE).astype(BF16)

    ckvn = _rms(ckv_ref[...], kvg_ref[...]).astype(BF16)
    for c in range(MLA_WIDTH // chunk):
        sl = slice(c * chunk, (c + 1) * chunk)
        sl_v = slice(MLA_WIDTH + c * chunk, MLA_WIDTH + (c + 1) * chunk)
        kn_ref[:, sl] = jnp.dot(ckvn, wkv_ref[:, sl], preferred_element_type=F32).astype(BF16)
        v_ref[:, sl] = jnp.dot(ckvn, wkv_ref[:, sl_v], preferred_element_type=F32).astype(BF16)

    kr = kr_ref[...]
    kr_rot = kr * cos + pltpu.roll(kr, 32, 1) * sin
    zero = jnp.zeros_like(kr_rot)
    kro_ref[:, :LANES] = jnp.where(lane < QK_ROPE_DIM, kr_rot, zero).astype(BF16)
    kro_ref[:, LANES:] = jnp.where(lane >= QK_ROPE_DIM, kr_rot, zero).astype(BF16)


def qkv_project(z, cos, sin, q_gain, kv_gain, wq, wkv, tm):
    n = z.shape[0]
    rope_w = N_HEADS * QK_ROPE_DIM
    row = lambda i: (i, 0)
    const = lambda i: (0, 0)
    return pl.pallas_call(
        _qkv_kernel,
        grid=(n // tm,),
        in_specs=[pl.BlockSpec((tm, Q_LORA_RANK), lambda i: (i, Z_Q // Q_LORA_RANK)),
                  pl.BlockSpec((tm, KV_LORA_RANK), lambda i: (i, Z_KV // KV_LORA_RANK)),
                  pl.BlockSpec((tm, LANES), lambda i: (i, Z_KR // LANES)),
                  pl.BlockSpec((tm, LANES), row),
                  pl.BlockSpec((tm, LANES), row),
                  pl.BlockSpec((1, Q_LORA_RANK), const),
                  pl.BlockSpec((1, KV_LORA_RANK), const),
                  pl.BlockSpec(wq.shape, const),
                  pl.BlockSpec(wkv.shape, const)],
        out_specs=[pl.BlockSpec((tm, MLA_WIDTH), row),
                   pl.BlockSpec((tm, rope_w), row),
                   pl.BlockSpec((tm, MLA_WIDTH), row),
                   pl.BlockSpec((tm, MLA_WIDTH), row),
                   pl.BlockSpec((tm, 2 * LANES), row)],
        out_shape=[jax.ShapeDtypeStruct((n, MLA_WIDTH), BF16),
                   jax.ShapeDtypeStruct((n, rope_w), BF16),
                   jax.ShapeDtypeStruct((n, MLA_WIDTH), BF16),
                   jax.ShapeDtypeStruct((n, MLA_WIDTH), BF16),
                   jax.ShapeDtypeStruct((n, 2 * LANES), BF16)],
        compiler_params=_params("parallel"),
        name="qkv_project",
    )(z, z, z, cos, sin, q_gain, kv_gain, wq, wkv)


def _online_softmax_update(q, k, v, state, mask):
    m_prev, l_prev, acc_prev = state
    s = lax.dot_general(q, k, (((1,), (1,)), ((), ())), preferred_element_type=F32)
    if mask is not None:
        s = jnp.where(mask, s, MASK_VALUE)
    m_next = jnp.maximum(m_prev, jnp.max(s, axis=1, keepdims=True))
    alpha = jnp.exp2(m_prev - m_next)
    ps = [jnp.exp2(s[:, c * LANES:(c + 1) * LANES] - m_next) for c in range(s.shape[1] // LANES)]
    l_next = alpha * l_prev + functools.reduce(lambda a, b: a + b, ps)
    p = jnp.concatenate(ps, axis=1).astype(BF16)
    acc_next = alpha * acc_prev + jnp.dot(p, v, preferred_element_type=F32)
    return m_next, l_next, acc_next


def _attn_kernel(qn_ref, qr_ref, kn_ref, kr_ref, v_ref, o_ref, m_scr, l_scr, acc_scr, *, tq, tk):
    i = pl.program_id(2)
    per_tile = tq // tk
    m_scr[...] = jnp.full(m_scr.shape, MASK_VALUE, F32)
    l_scr[...] = jnp.zeros(l_scr.shape, F32)
    acc_scr[...] = jnp.zeros(acc_scr.shape, F32)

    def load_q(rows):
        return jnp.concatenate([qn_ref[rows, :], qr_ref[rows, :]], axis=1)

    def load_kv(start):
        k = jnp.concatenate([kn_ref[pl.ds(start, tk), :], kr_ref[pl.ds(start, tk), :]], axis=1)
        return k, v_ref[pl.ds(start, tk), :]

    def load_state(rows):
        return m_scr[rows, :], l_scr[rows, :], acc_scr[rows, :]

    def store_state(rows, state):
        m_scr[rows, :], l_scr[rows, :], acc_scr[rows, :] = state

    everything = slice(0, tq)

    def body(g, carry):
        q = load_q(everything)
        state = load_state(everything)
        for u in range(per_tile):
            k, v = load_kv(pl.multiple_of((g * per_tile + u) * tk, tk))
            state = _online_softmax_update(q, k, v, state, None)
        store_state(everything, state)
        return carry

    lax.fori_loop(0, i, body, 0)

    for d in range(per_tile):
        rows = slice(d * tk, tq)
        nrows = tq - d * tk
        k, v = load_kv(pl.multiple_of(i * tq + d * tk, tk))
        row = lax.broadcasted_iota(jnp.int32, (nrows, tk), 0)
        col = lax.broadcasted_iota(jnp.int32, (nrows, tk), 1)
        state = _online_softmax_update(load_q(rows), k, v, load_state(rows), col <= row)
        store_state(rows, state)

    l = jnp.sum(l_scr[...], axis=1, keepdims=True)
    o_ref[...] = (acc_scr[...] / l).astype(o_ref.dtype)


def causal_attention(qn, qr, kn, kr, v, batch, seq, tq, tk):
    n = qn.shape[0]
    nq = seq // tq
    return pl.pallas_call(
        functools.partial(_attn_kernel, tq=tq, tk=tk),
        grid=(batch, N_HEADS, nq),
        in_specs=[pl.BlockSpec((tq, QK_NOPE_DIM), lambda b, h, i: (b * nq + i, h)),
                  pl.BlockSpec((tq, ROPE_PAIR_WIDTH), lambda b, h, i: (b * nq + i, h // 2)),
                  pl.BlockSpec((seq, QK_NOPE_DIM), lambda b, h, i: (b, h)),
                  pl.BlockSpec((seq, ROPE_PAIR_WIDTH), lambda b, h, i: (b, h % 2)),
                  pl.BlockSpec((seq, V_HEAD_DIM), lambda b, h, i: (b, h))],
        out_specs=pl.BlockSpec((tq, V_HEAD_DIM), lambda b, h, i: (b * nq + i, h)),
        out_shape=jax.ShapeDtypeStruct((n, MLA_WIDTH), BF16),
        scratch_shapes=[pltpu.VMEM((tq, LANES), F32),
                        pltpu.VMEM((tq, LANES), F32),
                        pltpu.VMEM((tq, V_HEAD_DIM), F32)],
        compiler_params=_params("parallel", "parallel", "arbitrary"),
        name="causal_attention",
    )(qn, qr, kn, kr, v)


def _pool_kernel(u_ref, halo_ref, w_ref, scale_ref, o_ref, *, ts):
    i = pl.program_id(1)
    t = i * ts + lax.broadcasted_iota(jnp.int32, (ts, 1), 0)
    for g, window in enumerate(POOL_WINDOWS):
        sl = slice(g * POOL_GROUP_DIM, (g + 1) * POOL_GROUP_DIM)
        x = u_ref[:, sl]
        halo = jnp.where(i > 0, halo_ref[:, sl], 0.0)
        s = jnp.concatenate([halo, x], axis=0)
        shift = 1
        while shift < window:
            s = s + pltpu.roll(s, shift, 0)
            shift *= 2
        cnt = jnp.minimum(t + 1, window).astype(F32)
        d = s[POOL_HALO:, :] / cnt - x
        y = jnp.dot(d.astype(BF16), w_ref[g], preferred_element_type=F32)
        o_ref[:, sl] = (y * scale_ref[:, sl]).astype(o_ref.dtype)


def multiscale_pool(z, pool_w, pool_scale, batch, seq, ts):
    n = z.shape[0]
    ns = seq // ts
    halo_blocks = ts // POOL_HALO
    return pl.pallas_call(
        functools.partial(_pool_kernel, ts=ts),
        grid=(batch, ns),
        in_specs=[pl.BlockSpec((ts, POOL_WIDTH), lambda b, i: (b * ns + i, Z_U // POOL_WIDTH)),
                  pl.BlockSpec((POOL_HALO, POOL_WIDTH),
                               lambda b, i: (jnp.maximum((b * ns + i) * halo_blocks - 1, 0), 0)),
                  pl.BlockSpec(pool_w.shape, lambda b, i: (0, 0, 0)),
                  pl.BlockSpec((1, POOL_WIDTH), lambda b, i: (0, 0))],
        out_specs=pl.BlockSpec((ts, POOL_WIDTH), lambda b, i: (b * ns + i, 0)),
        out_shape=jax.ShapeDtypeStruct((n, POOL_WIDTH), BF16),
        compiler_params=_params("parallel", "parallel"),
        name="multiscale_pool",
    )(z, z, pool_w, pool_scale)


def _resid_norm_kernel(m_ref, h_ref, gpost_ref, gnext_ref, ho_ref, ao_ref):
    h = h_ref[...] + _rms(m_ref[...], gpost_ref[...])
    ho_ref[...] = h
    ao_ref[...] = _rms(h, gnext_ref[...]).astype(BF16)


def _resid_kernel(m_ref, h_ref, gpost_ref, ho_ref):
    ho_ref[...] = h_ref[...] + _rms(m_ref[...], gpost_ref[...])


def resid_norm(m, h, g_post, g_next, rows):
    n, d = h.shape
    row = pl.BlockSpec((rows, d), lambda i: (i, 0))
    gain = pl.BlockSpec((1, d), lambda i: (0, 0))
    if g_next is None:
        return pl.pallas_call(
            _resid_kernel, grid=(n // rows,),
            in_specs=[row, row, gain], out_specs=row,
            out_shape=jax.ShapeDtypeStruct((n, d), F32),
            compiler_params=_params("parallel"), name="resid",
        )(m, h, g_post), None
    return pl.pallas_call(
        _resid_norm_kernel, grid=(n // rows,),
        in_specs=[row, row, gain, gain], out_specs=[row, row],
        out_shape=[jax.ShapeDtypeStruct((n, d), F32), jax.ShapeDtypeStruct((n, d), BF16)],
        compiler_params=_params("parallel"), name="resid_norm",
    )(m, h, g_post, g_next)


def _ffn_kernel(a_ref, wg_ref, wu_ref, wd_ref, o_ref, *, chunk):
    f = pl.program_id(1)
    a = a_ref[...]
    g = jnp.dot(a, wg_ref[...], preferred_element_type=F32)
    u = jnp.dot(a, wu_ref[...], preferred_element_type=F32)
    p = (g * (1.0 / (1.0 + jnp.exp(-g))) * u).astype(BF16)
    d = o_ref.shape[1]

    @pl.when(f == 0)
    def _():
        for c in range(d // chunk):
            sl = slice(c * chunk, (c + 1) * chunk)
            o_ref[:, sl] = jnp.dot(p, wd_ref[:, sl], preferred_element_type=F32)

    @pl.when(f > 0)
    def _():
        for c in range(d // chunk):
            sl = slice(c * chunk, (c + 1) * chunk)
            o_ref[:, sl] += jnp.dot(p, wd_ref[:, sl], preferred_element_type=F32)


def swiglu(a, w_gate, w_up, w_down, tm, tf):
    n, d = a.shape
    d_ff = w_gate.shape[1]
    return pl.pallas_call(
        functools.partial(_ffn_kernel, chunk=512),
        grid=(n // tm, d_ff // tf),
        in_specs=[pl.BlockSpec((tm, d), lambda i, f: (i, 0), pipeline_mode=pl.Buffered(1)),
                  pl.BlockSpec((d, tf), lambda i, f: (0, f)),
                  pl.BlockSpec((d, tf), lambda i, f: (0, f)),
                  pl.BlockSpec((tf, d), lambda i, f: (f, 0))],
        out_specs=pl.BlockSpec((tm, d), lambda i, f: (i, 0)),
        out_shape=jax.ShapeDtypeStruct((n, d), F32),
        compiler_params=_params("parallel", "arbitrary"),
        name="swiglu",
    )(a, w_gate, w_up, w_down)


def _prep_w_in(w_in):
    o1 = Q_LORA_RANK
    o2 = o1 + KV_LORA_RANK
    o3 = o2 + QK_ROPE_DIM
    w = w_in.astype(BF16)
    kr = w[:, o2:o3]
    return jnp.concatenate([w[:, o3:], w[:, o1:o2], kr, kr, w[:, :o1]], axis=-1)


def _prep_w_q(w_q_up):
    w = w_q_up.astype(BF16).reshape(Q_LORA_RANK, N_HEADS, QK_NOPE_DIM + QK_ROPE_DIM)
    nope = w[..., :QK_NOPE_DIM].reshape(Q_LORA_RANK, N_HEADS * QK_NOPE_DIM)
    rope = w[..., QK_NOPE_DIM:].reshape(Q_LORA_RANK, N_HEADS * QK_ROPE_DIM)
    return jnp.concatenate([nope, rope], axis=-1)


def _prep_w_kv(w_kv_up):
    w = w_kv_up.astype(BF16).reshape(KV_LORA_RANK, N_HEADS, QK_NOPE_DIM + V_HEAD_DIM)
    k_nope = w[..., :QK_NOPE_DIM].reshape(KV_LORA_RANK, N_HEADS * QK_NOPE_DIM)
    v = w[..., QK_NOPE_DIM:].reshape(KV_LORA_RANK, N_HEADS * V_HEAD_DIM)
    return jnp.concatenate([k_nope, v], axis=-1)


def kernel(x, positions, w_in, q_norm, w_q_up, kv_norm, w_kv_up, pool_w, pool_scale, w_out,
           norm_pre_mix, norm_post_mix, norm_pre_ffn, norm_post_ffn, w_gate, w_up, w_down):
    batch, seq, d_model = x.shape
    depth = w_in.shape[0]
    n = batch * seq

    tm_big = min(1024, n)
    tm_mid = min(512, n)
    rows_norm = min(256, n)
    tq = min(2048, seq)
    tk = min(512, seq)
    ts = min(512, seq)

    inv_freq = ROPE_THETA ** (-jnp.arange(0, QK_ROPE_DIM, 2, dtype=F32) / QK_ROPE_DIM)
    invf = jnp.tile(inv_freq, LANES // inv_freq.shape[0])[None, :]
    half = QK_ROPE_DIM // 2
    sign = jnp.where((jnp.arange(LANES) % QK_ROPE_DIM) < half, -1.0, 1.0).astype(F32)[None, :]
    cos, sin = rope_tables(positions.reshape(n, 1), invf, sign, min(2048, n))

    gain = lambda g, l: g[l][None, :]
    h = x.reshape(n, d_model)
    a = norm_cast(h, gain(norm_pre_mix, 0), rows_norm)
    for l in range(depth):
        z = matmul(a, _prep_w_in(w_in[l]), F32, tm_big, 512)
        qn, qr, kn, v, kr = qkv_project(z, cos, sin, gain(q_norm, l), gain(kv_norm, l),
                                        _prep_w_q(w_q_up[l]), _prep_w_kv(w_kv_up[l]), tm_mid)
        attn = causal_attention(qn, qr, kn, kr, v, batch, seq, tq, tk)
        pool = multiscale_pool(z, pool_w[l].astype(BF16), gain(pool_scale, l), batch, seq, ts)
        m = matmul_concat2(attn, pool, w_out[l].astype(BF16), tm_big, 512)
        h, a = resid_norm(m, h, gain(norm_post_mix, l), gain(norm_pre_ffn, l), rows_norm)
        f = swiglu(a, w_gate[l].astype(BF16), w_up[l].astype(BF16), w_down[l].astype(BF16), tm_big, 256)
        g_next = gain(norm_pre_mix, l + 1) if l + 1 < depth else None
        h, a = resid_norm(f, h, gain(norm_post_ffn, l), g_next, rows_norm)
    return h.reshape(batch, seq, d_model)
```

```python
import functools
import math

import jax
import jax.numpy as jnp
from jax import lax
from jax.experimental import pallas as pl
from jax.experimental.pallas import tpu as pltpu

F32 = jnp.float32
BF16 = jnp.bfloat16

N_HEADS = 16
QK_NOPE_DIM = 128
QK_ROPE_DIM = 64
V_HEAD_DIM = 128
Q_LORA_RANK = 896
KV_LORA_RANK = 512
POOL_WINDOWS = (2, 4, 8, 16)
POOL_GROUP_DIM = 512
POOL_WIDTH = POOL_GROUP_DIM * len(POOL_WINDOWS)
MLA_WIDTH = N_HEADS * V_HEAD_DIM
ROPE_PAIR_WIDTH = 2 * QK_ROPE_DIM
ATTN_SCALE = (QK_NOPE_DIM + QK_ROPE_DIM) ** -0.5
Q_PRESCALE = ATTN_SCALE * math.log2(math.e)
ROPE_THETA = 10000.0
NORM_EPS = 1e-6
MASK_VALUE = -1e30

LANES = 128
POOL_HALO = 16
VMEM_LIMIT_BYTES = 56 * 1024 * 1024

Z_U = 0
Z_KV = POOL_WIDTH
Z_KR = Z_KV + KV_LORA_RANK
Z_Q = Z_KR + 2 * QK_ROPE_DIM
Z_WIDTH = Z_Q + Q_LORA_RANK


def _params(*sem):
    return pltpu.CompilerParams(dimension_semantics=sem, vmem_limit_bytes=VMEM_LIMIT_BYTES)


def _rms(x, g):
    ms = jnp.mean(x * x, axis=-1, keepdims=True)
    return x * lax.rsqrt(ms + NORM_EPS) * g


def _rope_table_kernel(pos_ref, invf_ref, sign_ref, cos_ref, sin_ref):
    ang = pos_ref[...].astype(F32) * invf_ref[...]
    cos_ref[...] = jnp.cos(ang)
    sin_ref[...] = jnp.sin(ang) * sign_ref[...]


def rope_tables(pos_col, invf, sign, rows):
    n = pos_col.shape[0]
    return pl.pallas_call(
        _rope_table_kernel,
        grid=(n // rows,),
        in_specs=[pl.BlockSpec((rows, 1), lambda i: (i, 0)),
                  pl.BlockSpec((1, LANES), lambda i: (0, 0)),
                  pl.BlockSpec((1, LANES), lambda i: (0, 0))],
        out_specs=[pl.BlockSpec((rows, LANES), lambda i: (i, 0))] * 2,
        out_shape=[jax.ShapeDtypeStruct((n, LANES), F32)] * 2,
        compiler_params=_params("parallel"),
        name="rope_tables",
    )(pos_col, invf, sign)


def _norm_kernel(x_ref, g_ref, o_ref):
    o_ref[...] = _rms(x_ref[...], g_ref[...]).astype(BF16)


def norm_cast(x, g, rows):
    n, d = x.shape
    return pl.pallas_call(
        _norm_kernel,
        grid=(n // rows,),
        in_specs=[pl.BlockSpec((rows, d), lambda i: (i, 0)),
                  pl.BlockSpec((1, d), lambda i: (0, 0))],
        out_specs=pl.BlockSpec((rows, d), lambda i: (i, 0)),
        out_shape=jax.ShapeDtypeStruct((n, d), BF16),
        compiler_params=_params("parallel"),
        name="norm_cast",
    )(x, g)


def _mm_kernel(a_ref, w_ref, o_ref):
    o_ref[...] = jnp.dot(a_ref[...], w_ref[...], preferred_element_type=F32).astype(o_ref.dtype)


def matmul(a, w, layer, out_dtype, tm, tn):
    m, k = a.shape
    n = w.shape[2]
    return pl.pallas_call(
        _mm_kernel,
        grid=(m // tm, n // tn),
        in_specs=[pl.BlockSpec((tm, k), lambda i, j: (i, 0)),
                  pl.BlockSpec((None, k, tn), lambda i, j: (layer, 0, j))],
        out_specs=pl.BlockSpec((tm, tn), lambda i, j: (i, j)),
        out_shape=jax.ShapeDtypeStruct((m, n), out_dtype),
        compiler_params=_params("parallel", "arbitrary"),
        name="matmul",
    )(a, w)


def _mm2_kernel(a1_ref, a2_ref, w1_ref, w2_ref, o_ref):
    acc = jnp.dot(a1_ref[...], w1_ref[...], preferred_element_type=F32)
    acc += jnp.dot(a2_ref[...], w2_ref[...], preferred_element_type=F32)
    o_ref[...] = acc


def matmul_concat2(a1, a2, w, layer, tm, tn):
    m, k1 = a1.shape
    k2 = a2.shape[1]
    assert k1 == k2 and w.shape[1] == k1 + k2
    n = w.shape[2]
    return pl.pallas_call(
        _mm2_kernel,
        grid=(m // tm, n // tn),
        in_specs=[pl.BlockSpec((tm, k1), lambda i, j: (i, 0)),
                  pl.BlockSpec((tm, k2), lambda i, j: (i, 0)),
                  pl.BlockSpec((None, k1, tn), lambda i, j: (layer, 0, j)),
                  pl.BlockSpec((None, k2, tn), lambda i, j: (layer, 1, j))],
        out_specs=pl.BlockSpec((tm, tn), lambda i, j: (i, j)),
        out_shape=jax.ShapeDtypeStruct((m, n), F32),
        compiler_params=_params("parallel", "arbitrary"),
        name="matmul_concat2",
    )(a1, a2, w, w)


def _qkv_kernel(cq_ref, ckv_ref, kr_ref, cos_ref, sin_ref, qg_ref, kvg_ref, wq_ref, wkv_ref,
                qn_ref, qr_ref, kn_ref, v_ref, kro_ref):
    cos = cos_ref[...]
    sin = sin_ref[...]
    tm = cos.shape[0]
    lane = lax.broadcasted_iota(jnp.int32, (tm, LANES), 1)
    first_half = (lane % QK_ROPE_DIM) < (QK_ROPE_DIM // 2)
    chunk = 512

    cqn = _rms(cq_ref[...], qg_ref[...]).astype(BF16)
    for c in range(MLA_WIDTH // chunk):
        sl = slice(c * chunk, (c + 1) * chunk)
        qn = jnp.dot(cqn, wq_ref[:, sl], preferred_element_type=F32)
        qn_ref[:, sl] = (qn * Q_PRESCALE).astype(BF16)
    t_all = jnp.dot(cqn, wq_ref[:, MLA_WIDTH:], preferred_element_type=F32)
    for c in range(t_all.shape[1] // LANES):
        t = t_all[:, c * LANES:(c + 1) * LANES]
        swapped = jnp.where(first_half, pltpu.roll(t, LANES - 32, 1), pltpu.roll(t, 32, 1))
        qr_ref[:, c * LANES:(c + 1) * LANES] = ((t * cos + swapped * sin) * Q_PRESCALE).astype(BF16)

    ckvn = _rms(ckv_ref[...], kvg_ref[...]).astype(BF16)
    for c in range(MLA_WIDTH // chunk):
        sl = slice(c * chunk, (c + 1) * chunk)
        sl_v = slice(MLA_WIDTH + c * chunk, MLA_WIDTH + (c + 1) * chunk)
        kn_ref[:, sl] = jnp.dot(ckvn, wkv_ref[:, sl], preferred_element_type=F32).astype(BF16)
        v_ref[:, sl] = jnp.dot(ckvn, wkv_ref[:, sl_v], preferred_element_type=F32).astype(BF16)

    kr = kr_ref[...]
    kr_rot = kr * cos + pltpu.roll(kr, 32, 1) * sin
    zero = jnp.zeros_like(kr_rot)
    kro_ref[:, :LANES] = jnp.where(lane < QK_ROPE_DIM, kr_rot, zero).astype(BF16)
    kro_ref[:, LANES:] = jnp.where(lane >= QK_ROPE_DIM, kr_rot, zero).astype(BF16)


def qkv_project(z, cos, sin, q_gain, kv_gain, wq, wkv, layer, tm):
    n = z.shape[0]
    rope_w = N_HEADS * QK_ROPE_DIM
    row = lambda i: (i, 0)
    const = lambda i: (0, 0)
    weight = lambda i: (layer, 0, 0)
    return pl.pallas_call(
        _qkv_kernel,
        grid=(n // tm,),
        in_specs=[pl.BlockSpec((tm, Q_LORA_RANK), lambda i: (i, Z_Q // Q_LORA_RANK)),
                  pl.BlockSpec((tm, KV_LORA_RANK), lambda i: (i, Z_KV // KV_LORA_RANK)),
                  pl.BlockSpec((tm, LANES), lambda i: (i, Z_KR // LANES)),
                  pl.BlockSpec((tm, LANES), row),
                  pl.BlockSpec((tm, LANES), row),
                  pl.BlockSpec((1, Q_LORA_RANK), const),
                  pl.BlockSpec((1, KV_LORA_RANK), const),
                  pl.BlockSpec((None,) + wq.shape[1:], weight),
                  pl.BlockSpec((None,) + wkv.shape[1:], weight)],
        out_specs=[pl.BlockSpec((tm, MLA_WIDTH), row),
                   pl.BlockSpec((tm, rope_w), row),
                   pl.BlockSpec((tm, MLA_WIDTH), row),
                   pl.BlockSpec((tm, MLA_WIDTH), row),
                   pl.BlockSpec((tm, 2 * LANES), row)],
        out_shape=[jax.ShapeDtypeStruct((n, MLA_WIDTH), BF16),
                   jax.ShapeDtypeStruct((n, rope_w), BF16),
                   jax.ShapeDtypeStruct((n, MLA_WIDTH), BF16),
                   jax.ShapeDtypeStruct((n, MLA_WIDTH), BF16),
                   jax.ShapeDtypeStruct((n, 2 * LANES), BF16)],
        compiler_params=_params("parallel"),
        name="qkv_project",
    )(z, z, z, cos, sin, q_gain, kv_gain, wq, wkv)


def _online_softmax_update(q, k, v, state, mask):
    m_prev, l_prev, acc_prev = state
    s = lax.dot_general(q, k, (((1,), (1,)), ((), ())), preferred_element_type=F32)
    if mask is not None:
        s = jnp.where(mask, s, MASK_VALUE)
    m_next = jnp.maximum(m_prev, jnp.max(s, axis=1, keepdims=True))
    alpha = jnp.exp2(m_prev - m_next)
    ps = [jnp.exp2(s[:, c * LANES:(c + 1) * LANES] - m_next) for c in range(s.shape[1] // LANES)]
    l_next = alpha * l_prev + functools.reduce(lambda a, b: a + b, ps)
    p = jnp.concatenate(ps, axis=1).astype(BF16)
    acc_next = alpha * acc_prev + jnp.dot(p, v, preferred_element_type=F32)
    return m_next, l_next, acc_next


def _attn_kernel(qn_ref, qr_ref, kn_ref, kr_ref, v_ref, o_ref, m_scr, l_scr, acc_scr, *, tq, tk):
    i = pl.program_id(2)
    per_tile = tq // tk
    m_scr[...] = jnp.full(m_scr.shape, MASK_VALUE, F32)
    l_scr[...] = jnp.zeros(l_scr.shape, F32)
    acc_scr[...] = jnp.zeros(acc_scr.shape, F32)

    def load_q(rows):
        return jnp.concatenate([qn_ref[rows, :], qr_ref[rows, :]], axis=1)

    def load_kv(start):
        k = jnp.concatenate([kn_ref[pl.ds(start, tk), :], kr_ref[pl.ds(start, tk), :]], axis=1)
        return k, v_ref[pl.ds(start, tk), :]

    def load_state(rows):
        return m_scr[rows, :], l_scr[rows, :], acc_scr[rows, :]

    def store_state(rows, state):
        m_scr[rows, :], l_scr[rows, :], acc_scr[rows, :] = state

    everything = slice(0, tq)

    def body(g, carry):
        q = load_q(everything)
        state = load_state(everything)
        for u in range(per_tile):
            k, v = load_kv(pl.multiple_of((g * per_tile + u) * tk, tk))
            state = _online_softmax_update(q, k, v, state, None)
        store_state(everything, state)
        return carry

    lax.fori_loop(0, i, body, 0)

    for d in range(per_tile):
        rows = slice(d * tk, tq)
        nrows = tq - d * tk
        k, v = load_kv(pl.multiple_of(i * tq + d * tk, tk))
        row = lax.broadcasted_iota(jnp.int32, (nrows, tk), 0)
        col = lax.broadcasted_iota(jnp.int32, (nrows, tk), 1)
        state = _online_softmax_update(load_q(rows), k, v, load_state(rows), col <= row)
        store_state(rows, state)

    l = jnp.sum(l_scr[...], axis=1, keepdims=True)
    o_ref[...] = (acc_scr[...] / l).astype(o_ref.dtype)


def causal_attention(qn, qr, kn, kr, v, batch, seq, tq, tk):
    n = qn.shape[0]
    nq = seq // tq
    return pl.pallas_call(
        functools.partial(_attn_kernel, tq=tq, tk=tk),
        grid=(batch, N_HEADS, nq),
        in_specs=[pl.BlockSpec((tq, QK_NOPE_DIM), lambda b, h, i: (b * nq + i, h)),
                  pl.BlockSpec((tq, ROPE_PAIR_WIDTH), lambda b, h, i: (b * nq + i, h // 2)),
                  pl.BlockSpec((seq, QK_NOPE_DIM), lambda b, h, i: (b, h)),
                  pl.BlockSpec((seq, ROPE_PAIR_WIDTH), lambda b, h, i: (b, h % 2)),
                  pl.BlockSpec((seq, V_HEAD_DIM), lambda b, h, i: (b, h))],
        out_specs=pl.BlockSpec((tq, V_HEAD_DIM), lambda b, h, i: (b * nq + i, h)),
        out_shape=jax.ShapeDtypeStruct((n, MLA_WIDTH), BF16),
        scratch_shapes=[pltpu.VMEM((tq, LANES), F32),
                        pltpu.VMEM((tq, LANES), F32),
                        pltpu.VMEM((tq, V_HEAD_DIM), F32)],
        compiler_params=_params("parallel", "parallel", "arbitrary"),
        name="causal_attention",
    )(qn, qr, kn, kr, v)


def _pool_kernel(u_ref, halo_ref, w_ref, scale_ref, o_ref, *, ts):
    i = pl.program_id(1)
    t = i * ts + lax.broadcasted_iota(jnp.int32, (ts, 1), 0)
    for g, window in enumerate(POOL_WINDOWS):
        sl = slice(g * POOL_GROUP_DIM, (g + 1) * POOL_GROUP_DIM)
        x = u_ref[:, sl]
        halo = jnp.where(i > 0, halo_ref[:, sl], 0.0)
        s = jnp.concatenate([halo, x], axis=0)
        shift = 1
        while shift < window:
            s = s + pltpu.roll(s, shift, 0)
            shift *= 2
        cnt = jnp.minimum(t + 1, window).astype(F32)
        d = s[POOL_HALO:, :] / cnt - x
        y = jnp.dot(d.astype(BF16), w_ref[g], preferred_element_type=F32)
        o_ref[:, sl] = (y * scale_ref[:, sl]).astype(o_ref.dtype)


def multiscale_pool(z, pool_w, pool_scale, layer, batch, seq, ts):
    n = z.shape[0]
    ns = seq // ts
    halo_blocks = ts // POOL_HALO
    return pl.pallas_call(
        functools.partial(_pool_kernel, ts=ts),
        grid=(batch, ns),
        in_specs=[pl.BlockSpec((ts, POOL_WIDTH), lambda b, i: (b * ns + i, Z_U // POOL_WIDTH)),
                  pl.BlockSpec((POOL_HALO, POOL_WIDTH),
                               lambda b, i: (jnp.maximum((b * ns + i) * halo_blocks - 1, 0), 0)),
                  pl.BlockSpec((None,) + pool_w.shape[1:], lambda b, i: (layer, 0, 0, 0)),
                  pl.BlockSpec((1, POOL_WIDTH), lambda b, i: (0, 0))],
        out_specs=pl.BlockSpec((ts, POOL_WIDTH), lambda b, i: (b * ns + i, 0)),
        out_shape=jax.ShapeDtypeStruct((n, POOL_WIDTH), BF16),
        compiler_params=_params("parallel", "parallel"),
        name="multiscale_pool",
    )(z, z, pool_w, pool_scale)


def _resid_norm_kernel(m_ref, h_ref, gpost_ref, gnext_ref, ho_ref, ao_ref):
    h = h_ref[...] + _rms(m_ref[...], gpost_ref[...])
    ho_ref[...] = h
    ao_ref[...] = _rms(h, gnext_ref[...]).astype(BF16)


def _resid_kernel(m_ref, h_ref, gpost_ref, ho_ref):
    ho_ref[...] = h_ref[...] + _rms(m_ref[...], gpost_ref[...])


def resid_norm(m, h, g_post, g_next, rows):
    n, d = h.shape
    row = pl.BlockSpec((rows, d), lambda i: (i, 0))
    gain = pl.BlockSpec((1, d), lambda i: (0, 0))
    if g_next is None:
        return pl.pallas_call(
            _resid_kernel, grid=(n // rows,),
            in_specs=[row, row, gain], out_specs=row,
            out_shape=jax.ShapeDtypeStruct((n, d), F32),
            compiler_params=_params("parallel"), name="resid",
        )(m, h, g_post), None
    return pl.pallas_call(
        _resid_norm_kernel, grid=(n // rows,),
        in_specs=[row, row, gain, gain], out_specs=[row, row],
        out_shape=[jax.ShapeDtypeStruct((n, d), F32), jax.ShapeDtypeStruct((n, d), BF16)],
        compiler_params=_params("parallel"), name="resid_norm",
    )(m, h, g_post, g_next)


def _ffn_kernel(a_ref, wg_ref, wu_ref, wd_ref, o_ref, *, chunk):
    @pl.when(pl.program_id(1) == 0)
    def _():
        o_ref[...] = jnp.zeros(o_ref.shape, F32)

    a = a_ref[...]
    g = jnp.dot(a, wg_ref[...], preferred_element_type=F32)
    u = jnp.dot(a, wu_ref[...], preferred_element_type=F32)
    p = (g * (1.0 / (1.0 + jnp.exp(-g))) * u).astype(BF16)
    for c in range(o_ref.shape[1] // chunk):
        sl = slice(c * chunk, (c + 1) * chunk)
        o_ref[:, sl] += jnp.dot(p, wd_ref[:, sl], preferred_element_type=F32)


def swiglu(a, w_gate, w_up, w_down, layer, tm, tf):
    n, d = a.shape
    d_ff = w_gate.shape[2]
    return pl.pallas_call(
        functools.partial(_ffn_kernel, chunk=512),
        grid=(n // tm, d_ff // tf),
        in_specs=[pl.BlockSpec((tm, d), lambda i, f: (i, 0), pipeline_mode=pl.Buffered(1)),
                  pl.BlockSpec((None, d, tf), lambda i, f: (layer, 0, f)),
                  pl.BlockSpec((None, d, tf), lambda i, f: (layer, 0, f)),
                  pl.BlockSpec((None, tf, d), lambda i, f: (layer, f, 0))],
        out_specs=pl.BlockSpec((tm, d), lambda i, f: (i, 0)),
        out_shape=jax.ShapeDtypeStruct((n, d), F32),
        compiler_params=_params("parallel", "arbitrary"),
        name="swiglu",
    )(a, w_gate, w_up, w_down)


def _prep_w_in(w_in):
    o1 = Q_LORA_RANK
    o2 = o1 + KV_LORA_RANK
    o3 = o2 + QK_ROPE_DIM
    w = w_in.astype(BF16)
    kr = w[..., o2:o3]
    return jnp.concatenate([w[..., o3:], w[..., o1:o2], kr, kr, w[..., :o1]], axis=-1)


def _prep_w_q(w_q_up):
    depth = w_q_up.shape[0]
    w = w_q_up.astype(BF16).reshape(depth, Q_LORA_RANK, N_HEADS, QK_NOPE_DIM + QK_ROPE_DIM)
    nope = w[..., :QK_NOPE_DIM].reshape(depth, Q_LORA_RANK, N_HEADS * QK_NOPE_DIM)
    rope = w[..., QK_NOPE_DIM:].reshape(depth, Q_LORA_RANK, N_HEADS * QK_ROPE_DIM)
    return jnp.concatenate([nope, rope], axis=-1)


def _prep_w_kv(w_kv_up):
    depth = w_kv_up.shape[0]
    w = w_kv_up.astype(BF16).reshape(depth, KV_LORA_RANK, N_HEADS, QK_NOPE_DIM + V_HEAD_DIM)
    k_nope = w[..., :QK_NOPE_DIM].reshape(depth, KV_LORA_RANK, N_HEADS * QK_NOPE_DIM)
    v = w[..., QK_NOPE_DIM:].reshape(depth, KV_LORA_RANK, N_HEADS * V_HEAD_DIM)
    return jnp.concatenate([k_nope, v], axis=-1)


def kernel(x, positions, w_in, q_norm, w_q_up, kv_norm, w_kv_up, pool_w, pool_scale, w_out,
           norm_pre_mix, norm_post_mix, norm_pre_ffn, norm_post_ffn, w_gate, w_up, w_down):
    batch, seq, d_model = x.shape
    depth = w_in.shape[0]
    n = batch * seq

    tm_big = min(1024, n)
    tm_mid = min(512, n)
    rows_norm = min(256, n)
    tq = min(2048, seq)
    tk = min(512, seq)
    ts = min(512, seq)

    inv_freq = ROPE_THETA ** (-jnp.arange(0, QK_ROPE_DIM, 2, dtype=F32) / QK_ROPE_DIM)
    invf = jnp.tile(inv_freq, LANES // inv_freq.shape[0])[None, :]
    half = QK_ROPE_DIM // 2
    sign = jnp.where((jnp.arange(LANES) % QK_ROPE_DIM) < half, -1.0, 1.0).astype(F32)[None, :]
    cos, sin = rope_tables(positions.reshape(n, 1), invf, sign, min(2048, n))

    gain = lambda g, l: g[l][None, :]
    h = x.reshape(n, d_model)
    a = norm_cast(h, gain(norm_pre_mix, 0), rows_norm)
    w_in_b = _prep_w_in(w_in)
    w_q_b = _prep_w_q(w_q_up)
    w_kv_b = _prep_w_kv(w_kv_up)
    pool_w_b = pool_w.astype(BF16)
    w_out_b = w_out.astype(BF16)
    w_gate_b = w_gate.astype(BF16)
    w_up_b = w_up.astype(BF16)
    w_down_b = w_down.astype(BF16)
    for l in range(depth):
        z = matmul(a, w_in_b, l, F32, tm_big, 512)
        qn, qr, kn, v, kr = qkv_project(z, cos, sin, gain(q_norm, l), gain(kv_norm, l),
                                        w_q_b, w_kv_b, l, tm_mid)
        attn = causal_attention(qn, qr, kn, kr, v, batch, seq, tq, tk)
        pool = multiscale_pool(z, pool_w_b, gain(pool_scale, l), l, batch, seq, ts)
        m = matmul_concat2(attn, pool, w_out_b, l, tm_big, 512)
        h, a = resid_norm(m, h, gain(norm_post_mix, l), gain(norm_pre_ffn, l), rows_norm)
        f = swiglu(a, w_gate_b, w_up_b, w_down_b, l, tm_big, 256)
        g_next = gain(norm_pre_mix, l + 1) if l + 1 < depth else None
        h, a = resid_norm(f, h, gain(norm_post_ffn, l), g_next, rows_norm)
    return h.reshape(batch, seq, d_model)
```

```python
import functools
import math

import jax
import jax.numpy as jnp
from jax import lax
from jax.experimental import pallas as pl
from jax.experimental.pallas import tpu as pltpu

F32 = jnp.float32
BF16 = jnp.bfloat16

N_HEADS = 16
QK_NOPE_DIM = 128
QK_ROPE_DIM = 64
V_HEAD_DIM = 128
Q_LORA_RANK = 896
KV_LORA_RANK = 512
POOL_WINDOWS = (2, 4, 8, 16)
POOL_GROUP_DIM = 512
POOL_WIDTH = POOL_GROUP_DIM * len(POOL_WINDOWS)
MLA_WIDTH = N_HEADS * V_HEAD_DIM
ROPE_PAIR_WIDTH = 2 * QK_ROPE_DIM
ATTN_SCALE = (QK_NOPE_DIM + QK_ROPE_DIM) ** -0.5
Q_PRESCALE = ATTN_SCALE * math.log2(math.e)
ROPE_THETA = 10000.0
NORM_EPS = 1e-6
MASK_VALUE = -1e30

LANES = 128
BF16_SUBLANES = 16
POOL_HALO = 16
VMEM_LIMIT_BYTES = 56 * 1024 * 1024

Z_U = 0
Z_KV = POOL_WIDTH
Z_KR = Z_KV + KV_LORA_RANK
Z_Q = Z_KR + 2 * QK_ROPE_DIM
Z_WIDTH = Z_Q + Q_LORA_RANK


def _params(*sem):
    return pltpu.CompilerParams(dimension_semantics=sem, vmem_limit_bytes=VMEM_LIMIT_BYTES)


def _rms(x, g):
    ms = jnp.mean(x * x, axis=-1, keepdims=True)
    return x * lax.rsqrt(ms + NORM_EPS) * g


def _rope_table_kernel(pos_ref, invf_ref, sign_ref, cos_ref, sin_ref):
    ang = pos_ref[...].astype(F32) * invf_ref[...]
    cos_ref[...] = jnp.cos(ang)
    sin_ref[...] = jnp.sin(ang) * sign_ref[...]


def rope_tables(pos_col, invf, sign, rows):
    n = pos_col.shape[0]
    return pl.pallas_call(
        _rope_table_kernel,
        grid=(n // rows,),
        in_specs=[pl.BlockSpec((rows, 1), lambda i: (i, 0)),
                  pl.BlockSpec((1, LANES), lambda i: (0, 0)),
                  pl.BlockSpec((1, LANES), lambda i: (0, 0))],
        out_specs=[pl.BlockSpec((rows, LANES), lambda i: (i, 0))] * 2,
        out_shape=[jax.ShapeDtypeStruct((n, LANES), F32)] * 2,
        compiler_params=_params("parallel"),
        name="rope_tables",
    )(pos_col, invf, sign)


def _norm_kernel(x_ref, g_ref, o_ref):
    o_ref[...] = _rms(x_ref[...], g_ref[...]).astype(BF16)


def norm_cast(x, g, rows):
    n, d = x.shape
    return pl.pallas_call(
        _norm_kernel,
        grid=(n // rows,),
        in_specs=[pl.BlockSpec((rows, d), lambda i: (i, 0)),
                  pl.BlockSpec((1, d), lambda i: (0, 0))],
        out_specs=pl.BlockSpec((rows, d), lambda i: (i, 0)),
        out_shape=jax.ShapeDtypeStruct((n, d), BF16),
        compiler_params=_params("parallel"),
        name="norm_cast",
    )(x, g)


def _mm_kernel(a_ref, w_ref, o_ref):
    o_ref[...] = jnp.dot(a_ref[...], w_ref[...], preferred_element_type=F32).astype(o_ref.dtype)


def matmul(a, w, layer, out_dtype, tm, tn):
    m, k = a.shape
    n = w.shape[2]
    return pl.pallas_call(
        _mm_kernel,
        grid=(m // tm, n // tn),
        in_specs=[pl.BlockSpec((tm, k), lambda i, j: (i, 0)),
                  pl.BlockSpec((None, k, tn), lambda i, j: (layer, 0, j))],
        out_specs=pl.BlockSpec((tm, tn), lambda i, j: (i, j)),
        out_shape=jax.ShapeDtypeStruct((m, n), out_dtype),
        compiler_params=_params("parallel", "arbitrary"),
        name="matmul",
    )(a, w)


def _mm2_kernel(a1_ref, a2_ref, w1_ref, w2_ref, o_ref):
    acc = jnp.dot(a1_ref[...], w1_ref[...], preferred_element_type=F32)
    acc += jnp.dot(a2_ref[...], w2_ref[...], preferred_element_type=F32)
    o_ref[...] = acc


def matmul_concat2(a1, a2, w, layer, tm, tn):
    m, k1 = a1.shape
    k2 = a2.shape[1]
    assert k1 == k2 and w.shape[1] == k1 + k2
    n = w.shape[2]
    return pl.pallas_call(
        _mm2_kernel,
        grid=(m // tm, n // tn),
        in_specs=[pl.BlockSpec((tm, k1), lambda i, j: (i, 0)),
                  pl.BlockSpec((tm, k2), lambda i, j: (i, 0)),
                  pl.BlockSpec((None, k1, tn), lambda i, j: (layer, 0, j)),
                  pl.BlockSpec((None, k2, tn), lambda i, j: (layer, 1, j))],
        out_specs=pl.BlockSpec((tm, tn), lambda i, j: (i, j)),
        out_shape=jax.ShapeDtypeStruct((m, n), F32),
        compiler_params=_params("parallel", "arbitrary"),
        name="matmul_concat2",
    )(a1, a2, w, w)


def _qkv_kernel(cq_ref, ckv_ref, kr_ref, cos_ref, sin_ref, qg_ref, kvg_ref, wq_ref, wkv_ref,
                qn_ref, qr_ref, kn_ref, v_ref, kro_ref):
    cos = cos_ref[...]
    sin = sin_ref[...]
    tm = cos.shape[0]
    lane = lax.broadcasted_iota(jnp.int32, (tm, LANES), 1)
    first_half = (lane % QK_ROPE_DIM) < (QK_ROPE_DIM // 2)
    chunk = 512

    cqn = _rms(cq_ref[...], qg_ref[...]).astype(BF16)
    for c in range(MLA_WIDTH // chunk):
        sl = slice(c * chunk, (c + 1) * chunk)
        qn = jnp.dot(cqn, wq_ref[:, sl], preferred_element_type=F32)
        qn_ref[:, sl] = (qn * Q_PRESCALE).astype(BF16)
    t_all = jnp.dot(cqn, wq_ref[:, MLA_WIDTH:], preferred_element_type=F32)
    for c in range(t_all.shape[1] // LANES):
        t = t_all[:, c * LANES:(c + 1) * LANES]
        swapped = jnp.where(first_half, pltpu.roll(t, LANES - 32, 1), pltpu.roll(t, 32, 1))
        qr_ref[:, c * LANES:(c + 1) * LANES] = ((t * cos + swapped * sin) * Q_PRESCALE).astype(BF16)

    ckvn = _rms(ckv_ref[...], kvg_ref[...]).astype(BF16)
    for c in range(MLA_WIDTH // chunk):
        sl = slice(c * chunk, (c + 1) * chunk)
        sl_v = slice(MLA_WIDTH + c * chunk, MLA_WIDTH + (c + 1) * chunk)
        kn_ref[:, sl] = jnp.dot(ckvn, wkv_ref[:, sl], preferred_element_type=F32).astype(BF16)
        v_ref[:, sl] = jnp.dot(ckvn, wkv_ref[:, sl_v], preferred_element_type=F32).astype(BF16)

    kr = kr_ref[...]
    kr_rot = kr * cos + pltpu.roll(kr, 32, 1) * sin
    zero = jnp.zeros_like(kr_rot)
    kro_ref[:, :LANES] = jnp.where(lane < QK_ROPE_DIM, kr_rot, zero).astype(BF16)
    kro_ref[:, LANES:] = jnp.where(lane >= QK_ROPE_DIM, kr_rot, zero).astype(BF16)


def qkv_project(z, cos, sin, q_gain, kv_gain, wq, wkv, layer, tm):
    n = z.shape[0]
    rope_w = N_HEADS * QK_ROPE_DIM
    row = lambda i: (i, 0)
    const = lambda i: (0, 0)
    weight = lambda i: (layer, 0, 0)
    return pl.pallas_call(
        _qkv_kernel,
        grid=(n // tm,),
        in_specs=[pl.BlockSpec((tm, Q_LORA_RANK), lambda i: (i, Z_Q // Q_LORA_RANK)),
                  pl.BlockSpec((tm, KV_LORA_RANK), lambda i: (i, Z_KV // KV_LORA_RANK)),
                  pl.BlockSpec((tm, LANES), lambda i: (i, Z_KR // LANES)),
                  pl.BlockSpec((tm, LANES), row),
                  pl.BlockSpec((tm, LANES), row),
                  pl.BlockSpec((1, Q_LORA_RANK), const),
                  pl.BlockSpec((1, KV_LORA_RANK), const),
                  pl.BlockSpec((None,) + wq.shape[1:], weight),
                  pl.BlockSpec((None,) + wkv.shape[1:], weight)],
        out_specs=[pl.BlockSpec((tm, MLA_WIDTH), row),
                   pl.BlockSpec((tm, rope_w), row),
                   pl.BlockSpec((tm, MLA_WIDTH), row),
                   pl.BlockSpec((tm, MLA_WIDTH), row),
                   pl.BlockSpec((tm, 2 * LANES), row)],
        out_shape=[jax.ShapeDtypeStruct((n, MLA_WIDTH), BF16),
                   jax.ShapeDtypeStruct((n, rope_w), BF16),
                   jax.ShapeDtypeStruct((n, MLA_WIDTH), BF16),
                   jax.ShapeDtypeStruct((n, MLA_WIDTH), BF16),
                   jax.ShapeDtypeStruct((n, 2 * LANES), BF16)],
        compiler_params=_params("parallel"),
        name="qkv_project",
    )(z, z, z, cos, sin, q_gain, kv_gain, wq, wkv)


def _online_softmax_update(q, k, v, state, mask):
    m_prev, l_prev, acc_prev = state
    s = lax.dot_general(q, k, (((1,), (1,)), ((), ())), preferred_element_type=F32)
    if mask is not None:
        top = mask.shape[0]
        s_top = jnp.where(mask, s[:top], MASK_VALUE)
        s = s_top if s.shape[0] == top else jnp.concatenate([s_top, s[top:]], axis=0)
    m_next = jnp.maximum(m_prev, jnp.max(s, axis=1, keepdims=True))
    alpha = jnp.exp2(m_prev - m_next)
    ps = [jnp.exp2(s[:, c * LANES:(c + 1) * LANES] - m_next) for c in range(s.shape[1] // LANES)]
    l_next = alpha * l_prev + functools.reduce(lambda a, b: a + b, ps)
    p = jnp.concatenate(ps, axis=1).astype(BF16)
    acc_next = alpha * acc_prev + jnp.dot(p, v, preferred_element_type=F32)
    return m_next, l_next, acc_next


def _attn_kernel(qn_ref, qr_ref, kn_ref, kr_ref, v_ref, wg_ref, wu_ref, wd_ref,
                 o_ref, wgo_ref, wuo_ref, wdo_ref, m_scr, l_scr, acc_scr, *, tq, tk):
    wgo_ref[...] = wg_ref[...].astype(BF16)
    wuo_ref[...] = wu_ref[...].astype(BF16)
    wdo_ref[...] = wd_ref[...].astype(BF16)

    i = pl.program_id(2)
    per_tile = tq // tk
    m_scr[...] = jnp.full(m_scr.shape, MASK_VALUE, F32)
    l_scr[...] = jnp.zeros(l_scr.shape, F32)
    acc_scr[...] = jnp.zeros(acc_scr.shape, F32)

    def load_q(rows):
        return jnp.concatenate([qn_ref[rows, :], qr_ref[rows, :]], axis=1)

    def load_kv(start):
        k = jnp.concatenate([kn_ref[pl.ds(start, tk), :], kr_ref[pl.ds(start, tk), :]], axis=1)
        return k, v_ref[pl.ds(start, tk), :]

    def load_state(rows):
        return m_scr[rows, :], l_scr[rows, :], acc_scr[rows, :]

    def store_state(rows, state):
        m_scr[rows, :], l_scr[rows, :], acc_scr[rows, :] = state

    everything = slice(0, tq)

    def body(g, carry):
        q = load_q(everything)
        state = load_state(everything)
        for u in range(per_tile):
            k, v = load_kv(pl.multiple_of((g * per_tile + u) * tk, tk))
            state = _online_softmax_update(q, k, v, state, None)
        store_state(everything, state)
        return carry

    lax.fori_loop(0, i, body, 0)

    for d in range(per_tile):
        rows = slice(d * tk, tq)
        k, v = load_kv(pl.multiple_of(i * tq + d * tk, tk))
        row = lax.broadcasted_iota(jnp.int32, (tk, tk), 0)
        col = lax.broadcasted_iota(jnp.int32, (tk, tk), 1)
        state = _online_softmax_update(load_q(rows), k, v, load_state(rows), col <= row)
        store_state(rows, state)

    l = jnp.sum(l_scr[...], axis=1, keepdims=True)
    o_ref[...] = (acc_scr[...] / l).astype(o_ref.dtype)


def _slab_rows(total_rows, steps):
    rows = BF16_SUBLANES
    while total_rows % rows or total_rows // rows > steps:
        rows += BF16_SUBLANES
    return rows


def causal_attention(qn, qr, kn, kr, v, w_gate, w_up, w_down, layer, batch, seq, tq, tk):
    n = qn.shape[0]
    nq = seq // tq
    steps = batch * N_HEADS * nq
    _, d_model, d_ff = w_gate.shape
    rows_gu = _slab_rows(d_model, steps)
    rows_d = _slab_rows(d_ff, steps)

    def slab(nblocks):
        return lambda b, h, i: (jnp.minimum((b * N_HEADS + h) * nq + i, nblocks - 1), 0)

    def stacked_slab(nblocks):
        index = slab(nblocks)
        return lambda b, h, i: (layer,) + index(b, h, i)

    gu_blocks = d_model // rows_gu
    d_blocks = d_ff // rows_d
    return pl.pallas_call(
        functools.partial(_attn_kernel, tq=tq, tk=tk),
        grid=(batch, N_HEADS, nq),
        in_specs=[pl.BlockSpec((tq, QK_NOPE_DIM), lambda b, h, i: (b * nq + i, h)),
                  pl.BlockSpec((tq, ROPE_PAIR_WIDTH), lambda b, h, i: (b * nq + i, h // 2)),
                  pl.BlockSpec((seq, QK_NOPE_DIM), lambda b, h, i: (b, h)),
                  pl.BlockSpec((seq, ROPE_PAIR_WIDTH), lambda b, h, i: (b, h % 2)),
                  pl.BlockSpec((seq, V_HEAD_DIM), lambda b, h, i: (b, h)),
                  pl.BlockSpec((None, rows_gu, d_ff), stacked_slab(gu_blocks)),
                  pl.BlockSpec((None, rows_gu, d_ff), stacked_slab(gu_blocks)),
                  pl.BlockSpec((None, rows_d, d_model), stacked_slab(d_blocks))],
        out_specs=[pl.BlockSpec((tq, V_HEAD_DIM), lambda b, h, i: (b * nq + i, h)),
                   pl.BlockSpec((rows_gu, d_ff), slab(gu_blocks)),
                   pl.BlockSpec((rows_gu, d_ff), slab(gu_blocks)),
                   pl.BlockSpec((rows_d, d_model), slab(d_blocks))],
        out_shape=[jax.ShapeDtypeStruct((n, MLA_WIDTH), BF16),
                   jax.ShapeDtypeStruct((d_model, d_ff), BF16),
                   jax.ShapeDtypeStruct((d_model, d_ff), BF16),
                   jax.ShapeDtypeStruct((d_ff, d_model), BF16)],
        scratch_shapes=[pltpu.VMEM((tq, LANES), F32),
                        pltpu.VMEM((tq, LANES), F32),
                        pltpu.VMEM((tq, V_HEAD_DIM), F32)],
        compiler_params=_params("arbitrary", "arbitrary", "arbitrary"),
        name="causal_attention",
    )(qn, qr, kn, kr, v, w_gate, w_up, w_down)


def _pool_kernel(u_ref, halo_ref, w_ref, scale_ref, o_ref, *, ts):
    i = pl.program_id(1)
    t = i * ts + lax.broadcasted_iota(jnp.int32, (ts, 1), 0)
    for g, window in enumerate(POOL_WINDOWS):
        sl = slice(g * POOL_GROUP_DIM, (g + 1) * POOL_GROUP_DIM)
        x = u_ref[:, sl]
        halo = jnp.where(i > 0, halo_ref[:, sl], 0.0)
        s = jnp.concatenate([halo, x], axis=0)
        shift = 1
        while shift < window:
            s = s + pltpu.roll(s, shift, 0)
            shift *= 2
        cnt = jnp.minimum(t + 1, window).astype(F32)
        d = s[POOL_HALO:, :] / cnt - x
        y = jnp.dot(d.astype(BF16), w_ref[g], preferred_element_type=F32)
        o_ref[:, sl] = (y * scale_ref[:, sl]).astype(o_ref.dtype)


def multiscale_pool(z, pool_w, pool_scale, layer, batch, seq, ts):
    n = z.shape[0]
    ns = seq // ts
    halo_blocks = ts // POOL_HALO
    return pl.pallas_call(
        functools.partial(_pool_kernel, ts=ts),
        grid=(batch, ns),
        in_specs=[pl.BlockSpec((ts, POOL_WIDTH), lambda b, i: (b * ns + i, Z_U // POOL_WIDTH)),
                  pl.BlockSpec((POOL_HALO, POOL_WIDTH),
                               lambda b, i: (jnp.maximum((b * ns + i) * halo_blocks - 1, 0), 0)),
                  pl.BlockSpec((None,) + pool_w.shape[1:], lambda b, i: (layer, 0, 0, 0)),
                  pl.BlockSpec((1, POOL_WIDTH), lambda b, i: (0, 0))],
        out_specs=pl.BlockSpec((ts, POOL_WIDTH), lambda b, i: (b * ns + i, 0)),
        out_shape=jax.ShapeDtypeStruct((n, POOL_WIDTH), BF16),
        compiler_params=_params("parallel", "parallel"),
        name="multiscale_pool",
    )(z, z, pool_w, pool_scale)


def _resid_norm_kernel(m_ref, h_ref, gpost_ref, gnext_ref, ho_ref, ao_ref):
    h = h_ref[...] + _rms(m_ref[...], gpost_ref[...])
    ho_ref[...] = h
    ao_ref[...] = _rms(h, gnext_ref[...]).astype(BF16)


def _resid_kernel(m_ref, h_ref, gpost_ref, ho_ref):
    ho_ref[...] = h_ref[...] + _rms(m_ref[...], gpost_ref[...])


def resid_norm(m, h, g_post, g_next, rows):
    n, d = h.shape
    row = pl.BlockSpec((rows, d), lambda i: (i, 0))
    gain = pl.BlockSpec((1, d), lambda i: (0, 0))
    if g_next is None:
        return pl.pallas_call(
            _resid_kernel, grid=(n // rows,),
            in_specs=[row, row, gain], out_specs=row,
            out_shape=jax.ShapeDtypeStruct((n, d), F32),
            compiler_params=_params("parallel"), name="resid",
        )(m, h, g_post), None
    return pl.pallas_call(
        _resid_norm_kernel, grid=(n // rows,),
        in_specs=[row, row, gain, gain], out_specs=[row, row],
        out_shape=[jax.ShapeDtypeStruct((n, d), F32), jax.ShapeDtypeStruct((n, d), BF16)],
        compiler_params=_params("parallel"), name="resid_norm",
    )(m, h, g_post, g_next)


def _ffn_kernel(a_ref, wg_ref, wu_ref, wd_ref, o_ref, *, chunk):
    @pl.when(pl.program_id(1) == 0)
    def _():
        o_ref[...] = jnp.zeros(o_ref.shape, F32)

    a = a_ref[...]
    g = jnp.dot(a, wg_ref[...], preferred_element_type=F32)
    u = jnp.dot(a, wu_ref[...], preferred_element_type=F32)
    p = (g * (1.0 / (1.0 + jnp.exp(-g))) * u).astype(BF16)
    for c in range(o_ref.shape[1] // chunk):
        sl = slice(c * chunk, (c + 1) * chunk)
        o_ref[:, sl] += jnp.dot(p, wd_ref[:, sl], preferred_element_type=F32)


def swiglu(a, w_gate, w_up, w_down, tm, tf):
    n, d = a.shape
    d_ff = w_gate.shape[1]
    return pl.pallas_call(
        functools.partial(_ffn_kernel, chunk=512),
        grid=(n // tm, d_ff // tf),
        in_specs=[pl.BlockSpec((tm, d), lambda i, f: (i, 0), pipeline_mode=pl.Buffered(1)),
                  pl.BlockSpec((d, tf), lambda i, f: (0, f)),
                  pl.BlockSpec((d, tf), lambda i, f: (0, f)),
                  pl.BlockSpec((tf, d), lambda i, f: (f, 0))],
        out_specs=pl.BlockSpec((tm, d), lambda i, f: (i, 0)),
        out_shape=jax.ShapeDtypeStruct((n, d), F32),
        compiler_params=_params("parallel", "arbitrary"),
        name="swiglu",
    )(a, w_gate, w_up, w_down)


def _prep_w_in(w_in):
    o1 = Q_LORA_RANK
    o2 = o1 + KV_LORA_RANK
    o3 = o2 + QK_ROPE_DIM
    w = w_in.astype(BF16)
    kr = w[..., o2:o3]
    return jnp.concatenate([w[..., o3:], w[..., o1:o2], kr, kr, w[..., :o1]], axis=-1)


def _prep_w_q(w_q_up):
    depth = w_q_up.shape[0]
    w = w_q_up.astype(BF16).reshape(depth, Q_LORA_RANK, N_HEADS, QK_NOPE_DIM + QK_ROPE_DIM)
    nope = w[..., :QK_NOPE_DIM].reshape(depth, Q_LORA_RANK, N_HEADS * QK_NOPE_DIM)
    rope = w[..., QK_NOPE_DIM:].reshape(depth, Q_LORA_RANK, N_HEADS * QK_ROPE_DIM)
    return jnp.concatenate([nope, rope], axis=-1)


def _prep_w_kv(w_kv_up):
    depth = w_kv_up.shape[0]
    w = w_kv_up.astype(BF16).reshape(depth, KV_LORA_RANK, N_HEADS, QK_NOPE_DIM + V_HEAD_DIM)
    k_nope = w[..., :QK_NOPE_DIM].reshape(depth, KV_LORA_RANK, N_HEADS * QK_NOPE_DIM)
    v = w[..., QK_NOPE_DIM:].reshape(depth, KV_LORA_RANK, N_HEADS * V_HEAD_DIM)
    return jnp.concatenate([k_nope, v], axis=-1)


def kernel(x, positions, w_in, q_norm, w_q_up, kv_norm, w_kv_up, pool_w, pool_scale, w_out,
           norm_pre_mix, norm_post_mix, norm_pre_ffn, norm_post_ffn, w_gate, w_up, w_down):
    batch, seq, d_model = x.shape
    depth = w_in.shape[0]
    n = batch * seq

    tm_big = min(1024, n)
    tm_mid = min(512, n)
    rows_norm = min(256, n)
    tq = min(2048, seq)
    tk = min(512, seq)
    ts = min(512, seq)

    inv_freq = ROPE_THETA ** (-jnp.arange(0, QK_ROPE_DIM, 2, dtype=F32) / QK_ROPE_DIM)
    invf = jnp.tile(inv_freq, LANES // inv_freq.shape[0])[None, :]
    half = QK_ROPE_DIM // 2
    sign = jnp.where((jnp.arange(LANES) % QK_ROPE_DIM) < half, -1.0, 1.0).astype(F32)[None, :]
    cos, sin = rope_tables(positions.reshape(n, 1), invf, sign, min(2048, n))

    gain = lambda g, l: g[l][None, :]
    h = x.reshape(n, d_model)
    a = norm_cast(h, gain(norm_pre_mix, 0), rows_norm)
    w_in_b = _prep_w_in(w_in)
    w_q_b = _prep_w_q(w_q_up)
    w_kv_b = _prep_w_kv(w_kv_up)
    pool_w_b = pool_w.astype(BF16)
    w_out_b = w_out.astype(BF16)
    for l in range(depth):
        z = matmul(a, w_in_b, l, F32, tm_big, 512)
        qn, qr, kn, v, kr = qkv_project(z, cos, sin, gain(q_norm, l), gain(kv_norm, l),
                                        w_q_b, w_kv_b, l, tm_mid)
        attn, w_gate_b, w_up_b, w_down_b = causal_attention(qn, qr, kn, kr, v, w_gate, w_up, w_down, l,
                                                            batch, seq, tq, tk)
        pool = multiscale_pool(z, pool_w_b, gain(pool_scale, l), l, batch, seq, ts)
        m = matmul_concat2(attn, pool, w_out_b, l, tm_big, 512)
        h, a = resid_norm(m, h, gain(norm_post_mix, l), gain(norm_pre_ffn, l), rows_norm)
        f = swiglu(a, w_gate_b, w_up_b, w_down_b, tm_big, 256)
        g_next = gain(norm_pre_mix, l + 1) if l + 1 < depth else None
        h, a = resid_norm(f, h, gain(norm_post_ffn, l), g_next, rows_norm)
    return h.reshape(batch, seq, d_model)
```

```python
import functools
import math

import jax
import jax.numpy as jnp
from jax import lax
from jax.experimental import pallas as pl
from jax.experimental.pallas import tpu as pltpu

F32 = jnp.float32
BF16 = jnp.bfloat16

N_HEADS = 16
QK_NOPE_DIM = 128
QK_ROPE_DIM = 64
V_HEAD_DIM = 128
Q_LORA_RANK = 896
KV_LORA_RANK = 512
POOL_WINDOWS = (2, 4, 8, 16)
POOL_GROUP_DIM = 512
POOL_WIDTH = POOL_GROUP_DIM * len(POOL_WINDOWS)
MLA_WIDTH = N_HEADS * V_HEAD_DIM
ROPE_PAIR_WIDTH = 2 * QK_ROPE_DIM
ATTN_SCALE = (QK_NOPE_DIM + QK_ROPE_DIM) ** -0.5
Q_PRESCALE = ATTN_SCALE * math.log2(math.e)
ROPE_THETA = 10000.0
NORM_EPS = 1e-6
MASK_VALUE = -1e30

LANES = 128
BF16_SUBLANES = 16
POOL_HALO = 16
VMEM_LIMIT_BYTES = 56 * 1024 * 1024
TWO_ACC_VMEM_LIMIT_BYTES = 60 * 1024 * 1024

Z_U = 0
Z_KV = POOL_WIDTH
Z_KR = Z_KV + KV_LORA_RANK
Z_Q = Z_KR + 2 * QK_ROPE_DIM
Z_WIDTH = Z_Q + Q_LORA_RANK


def _params(*sem):
    return pltpu.CompilerParams(dimension_semantics=sem, vmem_limit_bytes=VMEM_LIMIT_BYTES)


def _rms(x, g):
    ms = jnp.mean(x * x, axis=-1, keepdims=True)
    return x * lax.rsqrt(ms + NORM_EPS) * g


def _rope_table_kernel(pos_ref, invf_ref, sign_ref, cos_ref, sin_ref):
    ang = pos_ref[...].astype(F32) * invf_ref[...]
    cos_ref[...] = jnp.cos(ang)
    sin_ref[...] = jnp.sin(ang) * sign_ref[...]


def rope_tables(pos_col, invf, sign, rows):
    n = pos_col.shape[0]
    return pl.pallas_call(
        _rope_table_kernel,
        grid=(n // rows,),
        in_specs=[pl.BlockSpec((rows, 1), lambda i: (i, 0)),
                  pl.BlockSpec((1, LANES), lambda i: (0, 0)),
                  pl.BlockSpec((1, LANES), lambda i: (0, 0))],
        out_specs=[pl.BlockSpec((rows, LANES), lambda i: (i, 0))] * 2,
        out_shape=[jax.ShapeDtypeStruct((n, LANES), F32)] * 2,
        compiler_params=_params("parallel"),
        name="rope_tables",
    )(pos_col, invf, sign)


def _norm_kernel(x_ref, g_ref, o_ref):
    o_ref[...] = _rms(x_ref[...], g_ref[...]).astype(BF16)


def norm_cast(x, g, rows):
    n, d = x.shape
    return pl.pallas_call(
        _norm_kernel,
        grid=(n // rows,),
        in_specs=[pl.BlockSpec((rows, d), lambda i: (i, 0)),
                  pl.BlockSpec((1, d), lambda i: (0, 0))],
        out_specs=pl.BlockSpec((rows, d), lambda i: (i, 0)),
        out_shape=jax.ShapeDtypeStruct((n, d), BF16),
        compiler_params=_params("parallel"),
        name="norm_cast",
    )(x, g)


def _mm_kernel(a_ref, w_ref, o_ref):
    o_ref[...] = jnp.dot(a_ref[...], w_ref[...], preferred_element_type=F32).astype(o_ref.dtype)


def matmul(a, w, layer, out_dtype, tm, tn):
    m, k = a.shape
    n = w.shape[2]
    return pl.pallas_call(
        _mm_kernel,
        grid=(m // tm, n // tn),
        in_specs=[pl.BlockSpec((tm, k), lambda i, j: (i, 0)),
                  pl.BlockSpec((None, k, tn), lambda i, j: (layer, 0, j))],
        out_specs=pl.BlockSpec((tm, tn), lambda i, j: (i, j)),
        out_shape=jax.ShapeDtypeStruct((m, n), out_dtype),
        compiler_params=_params("parallel", "arbitrary"),
        name="matmul",
    )(a, w)


def _resid_epilogue(read_rows, step, slabs, slab_rows, h_ref, gpost_ref, gnext_ref, ho_ref, ao_ref):
    slab = jnp.minimum(step, slabs - 1)
    rows = pl.ds(pl.multiple_of(slab * slab_rows, slab_rows), slab_rows)
    h = h_ref[...] + _rms(read_rows(rows), gpost_ref[...])
    ho_ref[...] = h
    if ao_ref is not None:
        ao_ref[...] = _rms(h, gnext_ref[...]).astype(BF16)


def _two_stream_step(i, nblocks, accs, matmuls, epilogue, prepare=None):
    for parity in (0, 1):
        @pl.when(jnp.logical_and(i < nblocks, i % 2 == parity))
        def _(parity=parity):
            if prepare is not None:
                prepare(accs[parity])
            epilogue(accs[1 - parity])
            matmuls(accs[parity])

    @pl.when(i == nblocks)
    def _():
        epilogue(accs[(nblocks - 1) % 2])


def _lagged_slab_index(slabs):
    def index(i, j):
        return (jnp.maximum((i - 1) * slabs + jnp.minimum(j, slabs - 1), 0), 0)
    return index


def _outproj_resid_kernel(a1_ref, a2_ref, w1_ref, w2_ref, h_ref, gpost_ref, gnext_ref,
                          ho_ref, ao_ref, acc0, acc1, *, nblocks, slabs, slab_rows):
    i = pl.program_id(0)
    j = pl.program_id(1)
    nsteps = acc0.shape[0]

    @pl.when(jnp.logical_and(i == 0, j == 0))
    def _():
        acc1[...] = jnp.zeros(acc1.shape, F32)

    def matmuls(acc):
        acc[j] = (jnp.dot(a1_ref[...], w1_ref[...], preferred_element_type=F32)
                  + jnp.dot(a2_ref[...], w2_ref[...], preferred_element_type=F32))

    def epilogue(acc):
        read_rows = lambda rows: jnp.concatenate([acc[c, rows, :] for c in range(nsteps)], axis=1)
        _resid_epilogue(read_rows, j, slabs, slab_rows, h_ref, gpost_ref, gnext_ref, ho_ref, ao_ref)

    _two_stream_step(i, nblocks, (acc0, acc1), matmuls, epilogue)


def outproj_resid(a1, a2, w, layer, h, g_post, g_next, tm, tn):
    m, k1 = a1.shape
    k2 = a2.shape[1]
    d = w.shape[2]
    assert k1 == k2 and w.shape[1] == k1 + k2
    nblocks = m // tm
    nsteps = d // tn
    slabs = nsteps
    slab_rows = tm // slabs
    assert tm % slabs == 0 and slab_rows % BF16_SUBLANES == 0

    block_row = lambda i, j: (jnp.minimum(i, nblocks - 1), 0)
    w_col = lambda i, j: jnp.where(i < nblocks, j, nsteps - 1)
    gain = pl.BlockSpec((1, d), lambda i, j: (0, 0))
    slab_spec = pl.BlockSpec((slab_rows, d), _lagged_slab_index(slabs))
    return pl.pallas_call(
        functools.partial(_outproj_resid_kernel, nblocks=nblocks, slabs=slabs, slab_rows=slab_rows),
        grid=(nblocks + 1, nsteps),
        in_specs=[pl.BlockSpec((tm, k1), block_row, pipeline_mode=pl.Buffered(1)),
                  pl.BlockSpec((tm, k2), block_row, pipeline_mode=pl.Buffered(1)),
                  pl.BlockSpec((None, k1, tn), lambda i, j: (layer, 0, w_col(i, j))),
                  pl.BlockSpec((None, k2, tn), lambda i, j: (layer, 1, w_col(i, j))),
                  slab_spec, gain, gain],
        out_specs=[slab_spec, slab_spec],
        out_shape=[jax.ShapeDtypeStruct((m, d), F32), jax.ShapeDtypeStruct((m, d), BF16)],
        scratch_shapes=[pltpu.VMEM((nsteps, tm, tn), F32), pltpu.VMEM((nsteps, tm, tn), F32)],
        compiler_params=pltpu.CompilerParams(dimension_semantics=("arbitrary", "arbitrary"),
                                             vmem_limit_bytes=TWO_ACC_VMEM_LIMIT_BYTES),
        name="outproj_resid",
    )(a1, a2, w, w, h, g_post, g_next)


def _qkv_kernel(cq_ref, ckv_ref, kr_ref, cos_ref, sin_ref, qg_ref, kvg_ref, wq_ref, wkv_ref,
                qn_ref, qr_ref, kn_ref, v_ref, kro_ref):
    cos = cos_ref[...]
    sin = sin_ref[...]
    tm = cos.shape[0]
    lane = lax.broadcasted_iota(jnp.int32, (tm, LANES), 1)
    first_half = (lane % QK_ROPE_DIM) < (QK_ROPE_DIM // 2)
    chunk = 512

    cqn = _rms(cq_ref[...], qg_ref[...]).astype(BF16)
    for c in range(MLA_WIDTH // chunk):
        sl = slice(c * chunk, (c + 1) * chunk)
        qn = jnp.dot(cqn, wq_ref[:, sl], preferred_element_type=F32)
        qn_ref[:, sl] = (qn * Q_PRESCALE).astype(BF16)
    t_all = jnp.dot(cqn, wq_ref[:, MLA_WIDTH:], preferred_element_type=F32)
    for c in range(t_all.shape[1] // LANES):
        t = t_all[:, c * LANES:(c + 1) * LANES]
        swapped = jnp.where(first_half, pltpu.roll(t, LANES - 32, 1), pltpu.roll(t, 32, 1))
        qr_ref[:, c * LANES:(c + 1) * LANES] = ((t * cos + swapped * sin) * Q_PRESCALE).astype(BF16)

    ckvn = _rms(ckv_ref[...], kvg_ref[...]).astype(BF16)
    for c in range(MLA_WIDTH // chunk):
        sl = slice(c * chunk, (c + 1) * chunk)
        sl_v = slice(MLA_WIDTH + c * chunk, MLA_WIDTH + (c + 1) * chunk)
        kn_ref[:, sl] = jnp.dot(ckvn, wkv_ref[:, sl], preferred_element_type=F32).astype(BF16)
        v_ref[:, sl] = jnp.dot(ckvn, wkv_ref[:, sl_v], preferred_element_type=F32).astype(BF16)

    kr = kr_ref[...]
    kr_rot = kr * cos + pltpu.roll(kr, 32, 1) * sin
    zero = jnp.zeros_like(kr_rot)
    kro_ref[:, :LANES] = jnp.where(lane < QK_ROPE_DIM, kr_rot, zero).astype(BF16)
    kro_ref[:, LANES:] = jnp.where(lane >= QK_ROPE_DIM, kr_rot, zero).astype(BF16)


def qkv_project(z, cos, sin, q_gain, kv_gain, wq, wkv, layer, tm):
    n = z.shape[0]
    rope_w = N_HEADS * QK_ROPE_DIM
    row = lambda i: (i, 0)
    const = lambda i: (0, 0)
    weight = lambda i: (layer, 0, 0)
    return pl.pallas_call(
        _qkv_kernel,
        grid=(n // tm,),
        in_specs=[pl.BlockSpec((tm, Q_LORA_RANK), lambda i: (i, Z_Q // Q_LORA_RANK)),
                  pl.BlockSpec((tm, KV_LORA_RANK), lambda i: (i, Z_KV // KV_LORA_RANK)),
                  pl.BlockSpec((tm, LANES), lambda i: (i, Z_KR // LANES)),
                  pl.BlockSpec((tm, LANES), row),
                  pl.BlockSpec((tm, LANES), row),
                  pl.BlockSpec((1, Q_LORA_RANK), const),
                  pl.BlockSpec((1, KV_LORA_RANK), const),
                  pl.BlockSpec((None,) + wq.shape[1:], weight),
                  pl.BlockSpec((None,) + wkv.shape[1:], weight)],
        out_specs=[pl.BlockSpec((tm, MLA_WIDTH), row),
                   pl.BlockSpec((tm, rope_w), row),
                   pl.BlockSpec((tm, MLA_WIDTH), row),
                   pl.BlockSpec((tm, MLA_WIDTH), row),
                   pl.BlockSpec((tm, 2 * LANES), row)],
        out_shape=[jax.ShapeDtypeStruct((n, MLA_WIDTH), BF16),
                   jax.ShapeDtypeStruct((n, rope_w), BF16),
                   jax.ShapeDtypeStruct((n, MLA_WIDTH), BF16),
                   jax.ShapeDtypeStruct((n, MLA_WIDTH), BF16),
                   jax.ShapeDtypeStruct((n, 2 * LANES), BF16)],
        compiler_params=_params("parallel"),
        name="qkv_project",
    )(z, z, z, cos, sin, q_gain, kv_gain, wq, wkv)


def _online_softmax_update(q, k, v, state, mask):
    m_prev, l_prev, acc_prev = state
    s = lax.dot_general(q, k, (((1,), (1,)), ((), ())), preferred_element_type=F32)
    if mask is not None:
        top = mask.shape[0]
        s_top = jnp.where(mask, s[:top], MASK_VALUE)
        s = s_top if s.shape[0] == top else jnp.concatenate([s_top, s[top:]], axis=0)
    m_next = jnp.maximum(m_prev, jnp.max(s, axis=1, keepdims=True))
    alpha = jnp.exp2(m_prev - m_next)
    ps = [jnp.exp2(s[:, c * LANES:(c + 1) * LANES] - m_next) for c in range(s.shape[1] // LANES)]
    l_next = alpha * l_prev + functools.reduce(lambda a, b: a + b, ps)
    p = jnp.concatenate(ps, axis=1).astype(BF16)
    acc_next = alpha * acc_prev + jnp.dot(p, v, preferred_element_type=F32)
    return m_next, l_next, acc_next


def _attn_kernel(qn_ref, qr_ref, kn_ref, kr_ref, v_ref, wg_ref, wu_ref, wd_ref,
                 o_ref, wgo_ref, wuo_ref, wdo_ref, m_scr, l_scr, acc_scr, *, tq, tk):
    wgo_ref[...] = wg_ref[...].astype(BF16)
    wuo_ref[...] = wu_ref[...].astype(BF16)
    wdo_ref[...] = wd_ref[...].astype(BF16)

    i = pl.program_id(2)
    per_tile = tq // tk
    m_scr[...] = jnp.full(m_scr.shape, MASK_VALUE, F32)
    l_scr[...] = jnp.zeros(l_scr.shape, F32)
    acc_scr[...] = jnp.zeros(acc_scr.shape, F32)

    def load_q(rows):
        return jnp.concatenate([qn_ref[rows, :], qr_ref[rows, :]], axis=1)

    def load_kv(start):
        k = jnp.concatenate([kn_ref[pl.ds(start, tk), :], kr_ref[pl.ds(start, tk), :]], axis=1)
        return k, v_ref[pl.ds(start, tk), :]

    def load_state(rows):
        return m_scr[rows, :], l_scr[rows, :], acc_scr[rows, :]

    def store_state(rows, state):
        m_scr[rows, :], l_scr[rows, :], acc_scr[rows, :] = state

    everything = slice(0, tq)

    def body(g, carry):
        q = load_q(everything)
        state = load_state(everything)
        for u in range(per_tile):
            k, v = load_kv(pl.multiple_of((g * per_tile + u) * tk, tk))
            state = _online_softmax_update(q, k, v, state, None)
        store_state(everything, state)
        return carry

    lax.fori_loop(0, i, body, 0)

    for d in range(per_tile):
        rows = slice(d * tk, tq)
        k, v = load_kv(pl.multiple_of(i * tq + d * tk, tk))
        row = lax.broadcasted_iota(jnp.int32, (tk, tk), 0)
        col = lax.broadcasted_iota(jnp.int32, (tk, tk), 1)
        state = _online_softmax_update(load_q(rows), k, v, load_state(rows), col <= row)
        store_state(rows, state)

    l = jnp.sum(l_scr[...], axis=1, keepdims=True)
    o_ref[...] = (acc_scr[...] / l).astype(o_ref.dtype)


def _slab_rows(total_rows, steps):
    rows = BF16_SUBLANES
    while total_rows % rows or total_rows // rows > steps:
        rows += BF16_SUBLANES
    return rows


def causal_attention(qn, qr, kn, kr, v, w_gate, w_up, w_down, layer, batch, seq, tq, tk):
    n = qn.shape[0]
    nq = seq // tq
    steps = batch * N_HEADS * nq
    _, d_model, d_ff = w_gate.shape
    rows_gu = _slab_rows(d_model, steps)
    rows_d = _slab_rows(d_ff, steps)

    def slab(nblocks):
        return lambda b, h, i: (jnp.minimum((b * N_HEADS + h) * nq + i, nblocks - 1), 0)

    def stacked_slab(nblocks):
        index = slab(nblocks)
        return lambda b, h, i: (layer,) + index(b, h, i)

    gu_blocks = d_model // rows_gu
    d_blocks = d_ff // rows_d
    return pl.pallas_call(
        functools.partial(_attn_kernel, tq=tq, tk=tk),
        grid=(batch, N_HEADS, nq),
        in_specs=[pl.BlockSpec((tq, QK_NOPE_DIM), lambda b, h, i: (b * nq + i, h)),
                  pl.BlockSpec((tq, ROPE_PAIR_WIDTH), lambda b, h, i: (b * nq + i, h // 2)),
                  pl.BlockSpec((seq, QK_NOPE_DIM), lambda b, h, i: (b, h)),
                  pl.BlockSpec((seq, ROPE_PAIR_WIDTH), lambda b, h, i: (b, h % 2)),
                  pl.BlockSpec((seq, V_HEAD_DIM), lambda b, h, i: (b, h)),
                  pl.BlockSpec((None, rows_gu, d_ff), stacked_slab(gu_blocks)),
                  pl.BlockSpec((None, rows_gu, d_ff), stacked_slab(gu_blocks)),
                  pl.BlockSpec((None, rows_d, d_model), stacked_slab(d_blocks))],
        out_specs=[pl.BlockSpec((tq, V_HEAD_DIM), lambda b, h, i: (b * nq + i, h)),
                   pl.BlockSpec((rows_gu, d_ff), slab(gu_blocks)),
                   pl.BlockSpec((rows_gu, d_ff), slab(gu_blocks)),
                   pl.BlockSpec((rows_d, d_model), slab(d_blocks))],
        out_shape=[jax.ShapeDtypeStruct((n, MLA_WIDTH), BF16),
                   jax.ShapeDtypeStruct((d_model, d_ff), BF16),
                   jax.ShapeDtypeStruct((d_model, d_ff), BF16),
                   jax.ShapeDtypeStruct((d_ff, d_model), BF16)],
        scratch_shapes=[pltpu.VMEM((tq, LANES), F32),
                        pltpu.VMEM((tq, LANES), F32),
                        pltpu.VMEM((tq, V_HEAD_DIM), F32)],
        compiler_params=_params("arbitrary", "arbitrary", "arbitrary"),
        name="causal_attention",
    )(qn, qr, kn, kr, v, w_gate, w_up, w_down)


def _pool_kernel(u_ref, halo_ref, w_ref, scale_ref, o_ref, *, ts):
    i = pl.program_id(1)
    t = i * ts + lax.broadcasted_iota(jnp.int32, (ts, 1), 0)
    for g, window in enumerate(POOL_WINDOWS):
        sl = slice(g * POOL_GROUP_DIM, (g + 1) * POOL_GROUP_DIM)
        x = u_ref[:, sl]
        halo = jnp.where(i > 0, halo_ref[:, sl], 0.0)
        s = jnp.concatenate([halo, x], axis=0)
        shift = 1
        while shift < window:
            s = s + pltpu.roll(s, shift, 0)
            shift *= 2
        cnt = jnp.minimum(t + 1, window).astype(F32)
        d = s[POOL_HALO:, :] / cnt - x
        y = jnp.dot(d.astype(BF16), w_ref[g], preferred_element_type=F32)
        o_ref[:, sl] = (y * scale_ref[:, sl]).astype(o_ref.dtype)


def multiscale_pool(z, pool_w, pool_scale, layer, batch, seq, ts):
    n = z.shape[0]
    ns = seq // ts
    halo_blocks = ts // POOL_HALO
    return pl.pallas_call(
        functools.partial(_pool_kernel, ts=ts),
        grid=(batch, ns),
        in_specs=[pl.BlockSpec((ts, POOL_WIDTH), lambda b, i: (b * ns + i, Z_U // POOL_WIDTH)),
                  pl.BlockSpec((POOL_HALO, POOL_WIDTH),
                               lambda b, i: (jnp.maximum((b * ns + i) * halo_blocks - 1, 0), 0)),
                  pl.BlockSpec((None,) + pool_w.shape[1:], lambda b, i: (layer, 0, 0, 0)),
                  pl.BlockSpec((1, POOL_WIDTH), lambda b, i: (0, 0))],
        out_specs=pl.BlockSpec((ts, POOL_WIDTH), lambda b, i: (b * ns + i, 0)),
        out_shape=jax.ShapeDtypeStruct((n, POOL_WIDTH), BF16),
        compiler_params=_params("parallel", "parallel"),
        name="multiscale_pool",
    )(z, z, pool_w, pool_scale)


def _ffn_resid_kernel(a_ref, wg_ref, wu_ref, wd_ref, h_ref, gpost_ref, *rest,
                      chunk, nblocks, slabs, slab_rows, with_next):
    if with_next:
        gnext_ref, ho_ref, ao_ref, acc0, acc1 = rest
    else:
        ho_ref, acc0, acc1 = rest
        gnext_ref = ao_ref = None
    i = pl.program_id(0)
    f = pl.program_id(1)

    @pl.when(jnp.logical_and(i == 0, f == 0))
    def _():
        acc1[...] = jnp.zeros(acc1.shape, F32)

    def prepare(acc):
        @pl.when(f == 0)
        def _():
            acc[...] = jnp.zeros(acc.shape, F32)

    def matmuls(acc):
        a = a_ref[...]
        g = jnp.dot(a, wg_ref[...], preferred_element_type=F32)
        u = jnp.dot(a, wu_ref[...], preferred_element_type=F32)
        p = (g * (1.0 / (1.0 + jnp.exp(-g))) * u).astype(BF16)
        for c in range(acc.shape[1] // chunk):
            sl = slice(c * chunk, (c + 1) * chunk)
            acc[:, sl] += jnp.dot(p, wd_ref[:, sl], preferred_element_type=F32)

    def epilogue(acc):
        _resid_epilogue(lambda rows: acc[rows, :], f, slabs, slab_rows,
                        h_ref, gpost_ref, gnext_ref, ho_ref, ao_ref)

    _two_stream_step(i, nblocks, (acc0, acc1), matmuls, epilogue, prepare)


def swiglu_resid(a, w_gate, w_up, w_down, h, g_post, g_next, tm, tf):
    n, d = a.shape
    d_ff = w_gate.shape[1]
    nblocks = n // tm
    nf = d_ff // tf
    slab_rows = 2 * BF16_SUBLANES
    slabs = tm // slab_rows
    assert slabs <= nf and tm % slab_rows == 0
    with_next = g_next is not None

    def f_index(i, f):
        return jnp.where(i < nblocks, f, nf - 1)

    gain = pl.BlockSpec((1, d), lambda i, f: (0, 0))
    slab_spec = pl.BlockSpec((slab_rows, d), _lagged_slab_index(slabs))
    in_specs = [pl.BlockSpec((tm, d), lambda i, f: (jnp.minimum(i, nblocks - 1), 0),
                             pipeline_mode=pl.Buffered(1)),
                pl.BlockSpec((d, tf), lambda i, f: (0, f_index(i, f))),
                pl.BlockSpec((d, tf), lambda i, f: (0, f_index(i, f))),
                pl.BlockSpec((tf, d), lambda i, f: (f_index(i, f), 0)),
                slab_spec, gain]
    args = [a, w_gate, w_up, w_down, h, g_post]
    out_specs = [slab_spec]
    out_shape = [jax.ShapeDtypeStruct((n, d), F32)]
    if with_next:
        in_specs.append(gain)
        args.append(g_next)
        out_specs.append(slab_spec)
        out_shape.append(jax.ShapeDtypeStruct((n, d), BF16))
    outs = pl.pallas_call(
        functools.partial(_ffn_resid_kernel, chunk=512, nblocks=nblocks, slabs=slabs,
                          slab_rows=slab_rows, with_next=with_next),
        grid=(nblocks + 1, nf),
        in_specs=in_specs,
        out_specs=out_specs,
        out_shape=out_shape,
        scratch_shapes=[pltpu.VMEM((tm, d), F32), pltpu.VMEM((tm, d), F32)],
        compiler_params=pltpu.CompilerParams(dimension_semantics=("arbitrary", "arbitrary"),
                                             vmem_limit_bytes=TWO_ACC_VMEM_LIMIT_BYTES),
        name="swiglu_resid",
    )(*args)
    return (outs[0], outs[1]) if with_next else (outs[0], None)


def _prep_w_in(w_in):
    o1 = Q_LORA_RANK
    o2 = o1 + KV_LORA_RANK
    o3 = o2 + QK_ROPE_DIM
    w = w_in.astype(BF16)
    kr = w[..., o2:o3]
    return jnp.concatenate([w[..., o3:], w[..., o1:o2], kr, kr, w[..., :o1]], axis=-1)


def _prep_w_q(w_q_up):
    depth = w_q_up.shape[0]
    w = w_q_up.astype(BF16).reshape(depth, Q_LORA_RANK, N_HEADS, QK_NOPE_DIM + QK_ROPE_DIM)
    nope = w[..., :QK_NOPE_DIM].reshape(depth, Q_LORA_RANK, N_HEADS * QK_NOPE_DIM)
    rope = w[..., QK_NOPE_DIM:].reshape(depth, Q_LORA_RANK, N_HEADS * QK_ROPE_DIM)
    return jnp.concatenate([nope, rope], axis=-1)


def _prep_w_kv(w_kv_up):
    depth = w_kv_up.shape[0]
    w = w_kv_up.astype(BF16).reshape(depth, KV_LORA_RANK, N_HEADS, QK_NOPE_DIM + V_HEAD_DIM)
    k_nope = w[..., :QK_NOPE_DIM].reshape(depth, KV_LORA_RANK, N_HEADS * QK_NOPE_DIM)
    v = w[..., QK_NOPE_DIM:].reshape(depth, KV_LORA_RANK, N_HEADS * V_HEAD_DIM)
    return jnp.concatenate([k_nope, v], axis=-1)


def kernel(x, positions, w_in, q_norm, w_q_up, kv_norm, w_kv_up, pool_w, pool_scale, w_out,
           norm_pre_mix, norm_post_mix, norm_pre_ffn, norm_post_ffn, w_gate, w_up, w_down):
    batch, seq, d_model = x.shape
    depth = w_in.shape[0]
    n = batch * seq

    tm_big = min(1024, n)
    tm_mid = min(512, n)
    rows_norm = min(256, n)
    rows_rope = min(2048, n)
    tq = min(2048, seq)
    tk = min(512, seq)
    ts = min(512, seq)
    assert all(n % t == 0 for t in (tm_big, tm_mid, rows_norm, rows_rope)), (batch, seq)
    assert seq % tq == 0 and tq % tk == 0 and seq % ts == 0 and ts % POOL_HALO == 0, seq

    inv_freq = ROPE_THETA ** (-jnp.arange(0, QK_ROPE_DIM, 2, dtype=F32) / QK_ROPE_DIM)
    invf = jnp.tile(inv_freq, LANES // inv_freq.shape[0])[None, :]
    half = QK_ROPE_DIM // 2
    sign = jnp.where((jnp.arange(LANES) % QK_ROPE_DIM) < half, -1.0, 1.0).astype(F32)[None, :]
    cos, sin = rope_tables(positions.reshape(n, 1), invf, sign, rows_rope)

    gain = lambda g, l: g[l][None, :]
    h = x.reshape(n, d_model)
    a = norm_cast(h, gain(norm_pre_mix, 0), rows_norm)
    w_in_b = _prep_w_in(w_in)
    w_q_b = _prep_w_q(w_q_up)
    w_kv_b = _prep_w_kv(w_kv_up)
    pool_w_b = pool_w.astype(BF16)
    w_out_b = w_out.astype(BF16)
    for l in range(depth):
        z = matmul(a, w_in_b, l, F32, tm_big, 512)
        qn, qr, kn, v, kr = qkv_project(z, cos, sin, gain(q_norm, l), gain(kv_norm, l),
                                        w_q_b, w_kv_b, l, tm_mid)
        attn, w_gate_b, w_up_b, w_down_b = causal_attention(qn, qr, kn, kr, v, w_gate, w_up, w_down, l,
                                                            batch, seq, tq, tk)
        pool = multiscale_pool(z, pool_w_b, gain(pool_scale, l), l, batch, seq, ts)
        h, a = outproj_resid(attn, pool, w_out_b, l, h, gain(norm_post_mix, l), gain(norm_pre_ffn, l),
                             tm_big, 256)
        g_next = gain(norm_pre_mix, l + 1) if l + 1 < depth else None
        h, a = swiglu_resid(a, w_gate_b, w_up_b, w_down_b, h, gain(norm_post_ffn, l), g_next, tm_big, 256)
    return h.reshape(batch, seq, d_model)
```

```python
import functools
import math

import jax
import jax.numpy as jnp
from jax import lax
from jax.experimental import pallas as pl
from jax.experimental.pallas import tpu as pltpu

F32 = jnp.float32
BF16 = jnp.bfloat16

N_HEADS = 16
QK_NOPE_DIM = 128
QK_ROPE_DIM = 64
V_HEAD_DIM = 128
Q_LORA_RANK = 896
KV_LORA_RANK = 512
POOL_WINDOWS = (2, 4, 8, 16)
POOL_GROUP_DIM = 512
POOL_WIDTH = POOL_GROUP_DIM * len(POOL_WINDOWS)
MLA_WIDTH = N_HEADS * V_HEAD_DIM
ROPE_PAIR_WIDTH = 2 * QK_ROPE_DIM
ATTN_SCALE = (QK_NOPE_DIM + QK_ROPE_DIM) ** -0.5
Q_PRESCALE = ATTN_SCALE * math.log2(math.e)
ROPE_THETA = 10000.0
NORM_EPS = 1e-6
MASK_VALUE = -1e30

LANES = 128
BF16_SUBLANES = 16
POOL_HALO = 16
VMEM_LIMIT_BYTES = 56 * 1024 * 1024
TWO_ACC_VMEM_LIMIT_BYTES = 60 * 1024 * 1024

Z_U = 0
Z_KV = POOL_WIDTH
Z_KR = Z_KV + KV_LORA_RANK
Z_Q = Z_KR + 2 * QK_ROPE_DIM
Z_WIDTH = Z_Q + Q_LORA_RANK


def _params(*sem):
    return pltpu.CompilerParams(dimension_semantics=sem, vmem_limit_bytes=VMEM_LIMIT_BYTES)


def _rms(x, g):
    ms = jnp.mean(x * x, axis=-1, keepdims=True)
    return x * lax.rsqrt(ms + NORM_EPS) * g


def _rope_table_kernel(pos_ref, invf_ref, sign_ref, cos_ref, sin_ref):
    ang = pos_ref[...].astype(F32) * invf_ref[...]
    cos_ref[...] = jnp.cos(ang)
    sin_ref[...] = jnp.sin(ang) * sign_ref[...]


def rope_tables(pos_col, invf, sign, rows):
    n = pos_col.shape[0]
    return pl.pallas_call(
        _rope_table_kernel,
        grid=(n // rows,),
        in_specs=[pl.BlockSpec((rows, 1), lambda i: (i, 0)),
                  pl.BlockSpec((1, LANES), lambda i: (0, 0)),
                  pl.BlockSpec((1, LANES), lambda i: (0, 0))],
        out_specs=[pl.BlockSpec((rows, LANES), lambda i: (i, 0))] * 2,
        out_shape=[jax.ShapeDtypeStruct((n, LANES), F32)] * 2,
        compiler_params=_params("parallel"),
        name="rope_tables",
    )(pos_col, invf, sign)


def _norm_kernel(x_ref, g_ref, o_ref):
    o_ref[...] = _rms(x_ref[...], g_ref[...]).astype(BF16)


def norm_cast(x, g, rows):
    n, d = x.shape
    return pl.pallas_call(
        _norm_kernel,
        grid=(n // rows,),
        in_specs=[pl.BlockSpec((rows, d), lambda i: (i, 0)),
                  pl.BlockSpec((1, d), lambda i: (0, 0))],
        out_specs=pl.BlockSpec((rows, d), lambda i: (i, 0)),
        out_shape=jax.ShapeDtypeStruct((n, d), BF16),
        compiler_params=_params("parallel"),
        name="norm_cast",
    )(x, g)


def _mm_kernel(a_ref, w_ref, o_ref):
    o_ref[...] = jnp.dot(a_ref[...], w_ref[...], preferred_element_type=F32).astype(o_ref.dtype)


def matmul(a, w, layer, out_dtype, tm, tn):
    m, k = a.shape
    n = w.shape[2]
    return pl.pallas_call(
        _mm_kernel,
        grid=(m // tm, n // tn),
        in_specs=[pl.BlockSpec((tm, k), lambda i, j: (i, 0)),
                  pl.BlockSpec((None, k, tn), lambda i, j: (layer, 0, j))],
        out_specs=pl.BlockSpec((tm, tn), lambda i, j: (i, j)),
        out_shape=jax.ShapeDtypeStruct((m, n), out_dtype),
        compiler_params=_params("parallel", "arbitrary"),
        name="matmul",
    )(a, w)


def _resid_epilogue(read_rows, step, slabs, slab_rows, h_ref, gpost_ref, gnext_ref, ho_ref, ao_ref):
    slab = jnp.minimum(step, slabs - 1)
    rows = pl.ds(pl.multiple_of(slab * slab_rows, slab_rows), slab_rows)
    h = h_ref[...] + _rms(read_rows(rows), gpost_ref[...])
    ho_ref[...] = h
    if ao_ref is not None:
        ao_ref[...] = _rms(h, gnext_ref[...]).astype(BF16)


def _two_stream_step(i, nblocks, accs, matmuls, epilogue, prepare=None):
    for parity in (0, 1):
        @pl.when(jnp.logical_and(i < nblocks, i % 2 == parity))
        def _(parity=parity):
            if prepare is not None:
                prepare(accs[parity])
            epilogue(accs[1 - parity])
            matmuls(accs[parity])

    @pl.when(i == nblocks)
    def _():
        epilogue(accs[(nblocks - 1) % 2])


def _lagged_slab_index(slabs):
    def index(i, j):
        return (jnp.maximum((i - 1) * slabs + jnp.minimum(j, slabs - 1), 0), 0)
    return index


def _outproj_resid_kernel(a1_ref, a2_ref, w1_ref, w2_ref, h_ref, gpost_ref, gnext_ref,
                          ho_ref, ao_ref, acc0, acc1, *, nblocks, slabs, slab_rows):
    i = pl.program_id(0)
    j = pl.program_id(1)
    nsteps = acc0.shape[0]

    @pl.when(jnp.logical_and(i == 0, j == 0))
    def _():
        acc1[...] = jnp.zeros(acc1.shape, F32)

    def matmuls(acc):
        acc[j] = (jnp.dot(a1_ref[...], w1_ref[...], preferred_element_type=F32)
                  + jnp.dot(a2_ref[...], w2_ref[...], preferred_element_type=F32))

    def epilogue(acc):
        read_rows = lambda rows: jnp.concatenate([acc[c, rows, :] for c in range(nsteps)], axis=1)
        _resid_epilogue(read_rows, j, slabs, slab_rows, h_ref, gpost_ref, gnext_ref, ho_ref, ao_ref)

    _two_stream_step(i, nblocks, (acc0, acc1), matmuls, epilogue)


def outproj_resid(a1, a2, w, layer, h, g_post, g_next, tm, tn):
    m, k1 = a1.shape
    k2 = a2.shape[1]
    d = w.shape[2]
    assert k1 == k2 and w.shape[1] == k1 + k2
    nblocks = m // tm
    nsteps = d // tn
    slabs = nsteps
    slab_rows = tm // slabs
    assert tm % slabs == 0 and slab_rows % BF16_SUBLANES == 0

    block_row = lambda i, j: (jnp.minimum(i, nblocks - 1), 0)
    w_col = lambda i, j: jnp.where(i < nblocks, j, nsteps - 1)
    gain = pl.BlockSpec((1, d), lambda i, j: (0, 0))
    slab_spec = pl.BlockSpec((slab_rows, d), _lagged_slab_index(slabs))
    return pl.pallas_call(
        functools.partial(_outproj_resid_kernel, nblocks=nblocks, slabs=slabs, slab_rows=slab_rows),
        grid=(nblocks + 1, nsteps),
        in_specs=[pl.BlockSpec((tm, k1), block_row),
                  pl.BlockSpec((tm, k2), block_row),
                  pl.BlockSpec((None, k1, tn), lambda i, j: (layer, 0, w_col(i, j))),
                  pl.BlockSpec((None, k2, tn), lambda i, j: (layer, 1, w_col(i, j))),
                  slab_spec, gain, gain],
        out_specs=[slab_spec, slab_spec],
        out_shape=[jax.ShapeDtypeStruct((m, d), F32), jax.ShapeDtypeStruct((m, d), BF16)],
        scratch_shapes=[pltpu.VMEM((nsteps, tm, tn), F32), pltpu.VMEM((nsteps, tm, tn), F32)],
        compiler_params=pltpu.CompilerParams(dimension_semantics=("arbitrary", "arbitrary"),
                                             vmem_limit_bytes=TWO_ACC_VMEM_LIMIT_BYTES),
        name="outproj_resid",
    )(a1, a2, w, w, h, g_post, g_next)


def _qkv_kernel(cq_ref, ckv_ref, kr_ref, cos_ref, sin_ref, qg_ref, kvg_ref, wq_ref, wkv_ref,
                qn_ref, qr_ref, kn_ref, v_ref, kro_ref):
    cos = cos_ref[...]
    sin = sin_ref[...]
    tm = cos.shape[0]
    lane = lax.broadcasted_iota(jnp.int32, (tm, LANES), 1)
    first_half = (lane % QK_ROPE_DIM) < (QK_ROPE_DIM // 2)
    chunk = 512

    cqn = _rms(cq_ref[...], qg_ref[...]).astype(BF16)
    for c in range(MLA_WIDTH // chunk):
        sl = slice(c * chunk, (c + 1) * chunk)
        qn = jnp.dot(cqn, wq_ref[:, sl], preferred_element_type=F32)
        qn_ref[:, sl] = (qn * Q_PRESCALE).astype(BF16)
    t_all = jnp.dot(cqn, wq_ref[:, MLA_WIDTH:], preferred_element_type=F32)
    for c in range(t_all.shape[1] // LANES):
        t = t_all[:, c * LANES:(c + 1) * LANES]
        swapped = jnp.where(first_half, pltpu.roll(t, LANES - 32, 1), pltpu.roll(t, 32, 1))
        qr_ref[:, c * LANES:(c + 1) * LANES] = ((t * cos + swapped * sin) * Q_PRESCALE).astype(BF16)

    ckvn = _rms(ckv_ref[...], kvg_ref[...]).astype(BF16)
    for c in range(MLA_WIDTH // chunk):
        sl = slice(c * chunk, (c + 1) * chunk)
        sl_v = slice(MLA_WIDTH + c * chunk, MLA_WIDTH + (c + 1) * chunk)
        kn_ref[:, sl] = jnp.dot(ckvn, wkv_ref[:, sl], preferred_element_type=F32).astype(BF16)
        v_ref[:, sl] = jnp.dot(ckvn, wkv_ref[:, sl_v], preferred_element_type=F32).astype(BF16)

    kr = kr_ref[...]
    kr_rot = kr * cos + pltpu.roll(kr, 32, 1) * sin
    zero = jnp.zeros_like(kr_rot)
    kro_ref[:, :LANES] = jnp.where(lane < QK_ROPE_DIM, kr_rot, zero).astype(BF16)
    kro_ref[:, LANES:] = jnp.where(lane >= QK_ROPE_DIM, kr_rot, zero).astype(BF16)


def qkv_project(z, cos, sin, q_gain, kv_gain, wq, wkv, layer, tm):
    n = z.shape[0]
    rope_w = N_HEADS * QK_ROPE_DIM
    row = lambda i: (i, 0)
    const = lambda i: (0, 0)
    weight = lambda i: (layer, 0, 0)
    return pl.pallas_call(
        _qkv_kernel,
        grid=(n // tm,),
        in_specs=[pl.BlockSpec((tm, Q_LORA_RANK), lambda i: (i, Z_Q // Q_LORA_RANK)),
                  pl.BlockSpec((tm, KV_LORA_RANK), lambda i: (i, Z_KV // KV_LORA_RANK)),
                  pl.BlockSpec((tm, LANES), lambda i: (i, Z_KR // LANES)),
                  pl.BlockSpec((tm, LANES), row),
                  pl.BlockSpec((tm, LANES), row),
                  pl.BlockSpec((1, Q_LORA_RANK), const),
                  pl.BlockSpec((1, KV_LORA_RANK), const),
                  pl.BlockSpec((None,) + wq.shape[1:], weight),
                  pl.BlockSpec((None,) + wkv.shape[1:], weight)],
        out_specs=[pl.BlockSpec((tm, MLA_WIDTH), row),
                   pl.BlockSpec((tm, rope_w), row),
                   pl.BlockSpec((tm, MLA_WIDTH), row),
                   pl.BlockSpec((tm, MLA_WIDTH), row),
                   pl.BlockSpec((tm, 2 * LANES), row)],
        out_shape=[jax.ShapeDtypeStruct((n, MLA_WIDTH), BF16),
                   jax.ShapeDtypeStruct((n, rope_w), BF16),
                   jax.ShapeDtypeStruct((n, MLA_WIDTH), BF16),
                   jax.ShapeDtypeStruct((n, MLA_WIDTH), BF16),
                   jax.ShapeDtypeStruct((n, 2 * LANES), BF16)],
        compiler_params=_params("parallel"),
        name="qkv_project",
    )(z, z, z, cos, sin, q_gain, kv_gain, wq, wkv)


def _scores(q, k):
    return lax.dot_general(q, k, (((1,), (1,)), ((), ())), preferred_element_type=F32)


def _online_softmax_update(s, v, state, mask):
    m_prev, l_prev, acc_prev = state
    if mask is not None:
        top = mask.shape[0]
        s_top = jnp.where(mask, s[:top], MASK_VALUE)
        s = s_top if s.shape[0] == top else jnp.concatenate([s_top, s[top:]], axis=0)
    m_next = jnp.maximum(m_prev, jnp.max(s, axis=1, keepdims=True))
    alpha = jnp.exp2(m_prev - m_next)
    ps = [jnp.exp2(s[:, c * LANES:(c + 1) * LANES] - m_next) for c in range(s.shape[1] // LANES)]
    l_next = alpha * l_prev + functools.reduce(lambda a, b: a + b, ps)
    p = jnp.concatenate(ps, axis=1).astype(BF16)
    acc_next = alpha * acc_prev + jnp.dot(p, v, preferred_element_type=F32)
    return m_next, l_next, acc_next


def _attn_kernel(qn_ref, qr_ref, kn_ref, kr_ref, v_ref, wg_ref, wu_ref, wd_ref,
                 o_ref, wgo_ref, wuo_ref, wdo_ref, m_scr, l_scr, acc_scr, *, tq, tk):
    wgo_ref[...] = wg_ref[...].astype(BF16)
    wuo_ref[...] = wu_ref[...].astype(BF16)
    wdo_ref[...] = wd_ref[...].astype(BF16)

    i = pl.program_id(2)
    per_tile = tq // tk
    m_scr[...] = jnp.full(m_scr.shape, MASK_VALUE, F32)
    l_scr[...] = jnp.zeros(l_scr.shape, F32)
    acc_scr[...] = jnp.zeros(acc_scr.shape, F32)

    def load_q(rows):
        return jnp.concatenate([qn_ref[rows, :], qr_ref[rows, :]], axis=1)

    def load_kv(start):
        k = jnp.concatenate([kn_ref[pl.ds(start, tk), :], kr_ref[pl.ds(start, tk), :]], axis=1)
        return k, v_ref[pl.ds(start, tk), :]

    def load_state(rows):
        return m_scr[rows, :], l_scr[rows, :], acc_scr[rows, :]

    def store_state(rows, state):
        m_scr[rows, :], l_scr[rows, :], acc_scr[rows, :] = state

    everything = slice(0, tq)

    def body(g, carry):
        q = load_q(everything)
        state = load_state(everything)
        for u in range(per_tile):
            k, v = load_kv(pl.multiple_of((g * per_tile + u) * tk, tk))
            state = _online_softmax_update(_scores(q, k), v, state, None)
        store_state(everything, state)
        return carry

    lax.fori_loop(0, i, body, 0)

    for d in range(per_tile):
        rows = slice(d * tk, tq)
        k, v = load_kv(pl.multiple_of(i * tq + d * tk, tk))
        row = lax.broadcasted_iota(jnp.int32, (tk, tk), 0)
        col = lax.broadcasted_iota(jnp.int32, (tk, tk), 1)
        state = _online_softmax_update(_scores(load_q(rows), k), v, load_state(rows), col <= row)
        store_state(rows, state)

    l = jnp.sum(l_scr[...], axis=1, keepdims=True)
    o_ref[...] = (acc_scr[...] / l).astype(o_ref.dtype)


def _slab_rows(total_rows, steps):
    rows = BF16_SUBLANES
    while total_rows % rows or total_rows // rows > steps:
        rows += BF16_SUBLANES
    return rows


def causal_attention(qn, qr, kn, kr, v, w_gate, w_up, w_down, layer, batch, seq, tq, tk):
    n = qn.shape[0]
    nq = seq // tq
    steps = batch * N_HEADS * nq
    _, d_model, d_ff = w_gate.shape
    rows_gu = _slab_rows(d_model, steps)
    rows_d = _slab_rows(d_ff, steps)

    def slab(nblocks):
        return lambda b, h, i: (jnp.minimum((b * N_HEADS + h) * nq + i, nblocks - 1), 0)

    def stacked_slab(nblocks):
        index = slab(nblocks)
        return lambda b, h, i: (layer,) + index(b, h, i)

    gu_blocks = d_model // rows_gu
    d_blocks = d_ff // rows_d
    return pl.pallas_call(
        functools.partial(_attn_kernel, tq=tq, tk=tk),
        grid=(batch, N_HEADS, nq),
        in_specs=[pl.BlockSpec((tq, QK_NOPE_DIM), lambda b, h, i: (b * nq + i, h)),
                  pl.BlockSpec((tq, ROPE_PAIR_WIDTH), lambda b, h, i: (b * nq + i, h // 2)),
                  pl.BlockSpec((seq, QK_NOPE_DIM), lambda b, h, i: (b, h)),
                  pl.BlockSpec((seq, ROPE_PAIR_WIDTH), lambda b, h, i: (b, h % 2)),
                  pl.BlockSpec((seq, V_HEAD_DIM), lambda b, h, i: (b, h)),
                  pl.BlockSpec((None, rows_gu, d_ff), stacked_slab(gu_blocks)),
                  pl.BlockSpec((None, rows_gu, d_ff), stacked_slab(gu_blocks)),
                  pl.BlockSpec((None, rows_d, d_model), stacked_slab(d_blocks))],
        out_specs=[pl.BlockSpec((tq, V_HEAD_DIM), lambda b, h, i: (b * nq + i, h)),
                   pl.BlockSpec((rows_gu, d_ff), slab(gu_blocks)),
                   pl.BlockSpec((rows_gu, d_ff), slab(gu_blocks)),
                   pl.BlockSpec((rows_d, d_model), slab(d_blocks))],
        out_shape=[jax.ShapeDtypeStruct((n, MLA_WIDTH), BF16),
                   jax.ShapeDtypeStruct((d_model, d_ff), BF16),
                   jax.ShapeDtypeStruct((d_model, d_ff), BF16),
                   jax.ShapeDtypeStruct((d_ff, d_model), BF16)],
        scratch_shapes=[pltpu.VMEM((tq, LANES), F32),
                        pltpu.VMEM((tq, LANES), F32),
                        pltpu.VMEM((tq, V_HEAD_DIM), F32)],
        compiler_params=_params("arbitrary", "arbitrary", "arbitrary"),
        name="causal_attention",
    )(qn, qr, kn, kr, v, w_gate, w_up, w_down)


def _pool_kernel(u_ref, halo_ref, w_ref, scale_ref, o_ref, *, ts):
    i = pl.program_id(1)
    t = i * ts + lax.broadcasted_iota(jnp.int32, (ts, 1), 0)
    for g, window in enumerate(POOL_WINDOWS):
        sl = slice(g * POOL_GROUP_DIM, (g + 1) * POOL_GROUP_DIM)
        x = u_ref[:, sl]
        halo = jnp.where(i > 0, halo_ref[:, sl], 0.0)
        s = jnp.concatenate([halo, x], axis=0)
        shift = 1
        while shift < window:
            s = s + pltpu.roll(s, shift, 0)
            shift *= 2
        cnt = jnp.minimum(t + 1, window).astype(F32)
        d = s[POOL_HALO:, :] / cnt - x
        y = jnp.dot(d.astype(BF16), w_ref[g], preferred_element_type=F32)
        o_ref[:, sl] = (y * scale_ref[:, sl]).astype(o_ref.dtype)


def multiscale_pool(z, pool_w, pool_scale, layer, batch, seq, ts):
    n = z.shape[0]
    ns = seq // ts
    halo_blocks = ts // POOL_HALO
    return pl.pallas_call(
        functools.partial(_pool_kernel, ts=ts),
        grid=(batch, ns),
        in_specs=[pl.BlockSpec((ts, POOL_WIDTH), lambda b, i: (b * ns + i, Z_U // POOL_WIDTH)),
                  pl.BlockSpec((POOL_HALO, POOL_WIDTH),
                               lambda b, i: (jnp.maximum((b * ns + i) * halo_blocks - 1, 0), 0)),
                  pl.BlockSpec((None,) + pool_w.shape[1:], lambda b, i: (layer, 0, 0, 0)),
                  pl.BlockSpec((1, POOL_WIDTH), lambda b, i: (0, 0))],
        out_specs=pl.BlockSpec((ts, POOL_WIDTH), lambda b, i: (b * ns + i, 0)),
        out_shape=jax.ShapeDtypeStruct((n, POOL_WIDTH), BF16),
        compiler_params=_params("parallel", "parallel"),
        name="multiscale_pool",
    )(z, z, pool_w, pool_scale)


def _ffn_resid_kernel(a_ref, wg_ref, wu_ref, wd_ref, h_ref, gpost_ref, *rest,
                      chunk, nblocks, slabs, slab_rows, with_next):
    if with_next:
        gnext_ref, ho_ref, ao_ref, acc0, acc1 = rest
    else:
        ho_ref, acc0, acc1 = rest
        gnext_ref = ao_ref = None
    i = pl.program_id(0)
    f = pl.program_id(1)

    @pl.when(jnp.logical_and(i == 0, f == 0))
    def _():
        acc1[...] = jnp.zeros(acc1.shape, F32)

    def prepare(acc):
        @pl.when(f == 0)
        def _():
            acc[...] = jnp.zeros(acc.shape, F32)

    def matmuls(acc):
        a = a_ref[...]
        g = jnp.dot(a, wg_ref[...], preferred_element_type=F32)
        u = jnp.dot(a, wu_ref[...], preferred_element_type=F32)
        p = (g * (1.0 / (1.0 + jnp.exp(-g))) * u).astype(BF16)
        for c in range(acc.shape[1] // chunk):
            sl = slice(c * chunk, (c + 1) * chunk)
            acc[:, sl] += jnp.dot(p, wd_ref[:, sl], preferred_element_type=F32)

    def epilogue(acc):
        _resid_epilogue(lambda rows: acc[rows, :], f, slabs, slab_rows,
                        h_ref, gpost_ref, gnext_ref, ho_ref, ao_ref)

    _two_stream_step(i, nblocks, (acc0, acc1), matmuls, epilogue, prepare)


def swiglu_resid(a, w_gate, w_up, w_down, h, g_post, g_next, tm, tf):
    n, d = a.shape
    d_ff = w_gate.shape[1]
    nblocks = n // tm
    nf = d_ff // tf
    slab_rows = 2 * BF16_SUBLANES
    slabs = tm // slab_rows
    assert slabs <= nf and tm % slab_rows == 0
    with_next = g_next is not None

    def f_index(i, f):
        return jnp.where(i < nblocks, f, nf - 1)

    gain = pl.BlockSpec((1, d), lambda i, f: (0, 0))
    slab_spec = pl.BlockSpec((slab_rows, d), _lagged_slab_index(slabs))
    in_specs = [pl.BlockSpec((tm, d), lambda i, f: (jnp.minimum(i, nblocks - 1), 0),
                             pipeline_mode=pl.Buffered(1)),
                pl.BlockSpec((d, tf), lambda i, f: (0, f_index(i, f))),
                pl.BlockSpec((d, tf), lambda i, f: (0, f_index(i, f))),
                pl.BlockSpec((tf, d), lambda i, f: (f_index(i, f), 0)),
                slab_spec, gain]
    args = [a, w_gate, w_up, w_down, h, g_post]
    out_specs = [slab_spec]
    out_shape = [jax.ShapeDtypeStruct((n, d), F32)]
    if with_next:
        in_specs.append(gain)
        args.append(g_next)
        out_specs.append(slab_spec)
        out_shape.append(jax.ShapeDtypeStruct((n, d), BF16))
    outs = pl.pallas_call(
        functools.partial(_ffn_resid_kernel, chunk=512, nblocks=nblocks, slabs=slabs,
                          slab_rows=slab_rows, with_next=with_next),
        grid=(nblocks + 1, nf),
        in_specs=in_specs,
        out_specs=out_specs,
        out_shape=out_shape,
        scratch_shapes=[pltpu.VMEM((tm, d), F32), pltpu.VMEM((tm, d), F32)],
        compiler_params=pltpu.CompilerParams(dimension_semantics=("arbitrary", "arbitrary"),
                                             vmem_limit_bytes=TWO_ACC_VMEM_LIMIT_BYTES),
        name="swiglu_resid",
    )(*args)
    return (outs[0], outs[1]) if with_next else (outs[0], None)


def _prep_w_in(w_in):
    o1 = Q_LORA_RANK
    o2 = o1 + KV_LORA_RANK
    o3 = o2 + QK_ROPE_DIM
    w = w_in.astype(BF16)
    kr = w[..., o2:o3]
    return jnp.concatenate([w[..., o3:], w[..., o1:o2], kr, kr, w[..., :o1]], axis=-1)


def _prep_w_q(w_q_up):
    depth = w_q_up.shape[0]
    w = w_q_up.astype(BF16).reshape(depth, Q_LORA_RANK, N_HEADS, QK_NOPE_DIM + QK_ROPE_DIM)
    nope = w[..., :QK_NOPE_DIM].reshape(depth, Q_LORA_RANK, N_HEADS * QK_NOPE_DIM)
    rope = w[..., QK_NOPE_DIM:].reshape(depth, Q_LORA_RANK, N_HEADS * QK_ROPE_DIM)
    return jnp.concatenate([nope, rope], axis=-1)


def _prep_w_kv(w_kv_up):
    depth = w_kv_up.shape[0]
    w = w_kv_up.astype(BF16).reshape(depth, KV_LORA_RANK, N_HEADS, QK_NOPE_DIM + V_HEAD_DIM)
    k_nope = w[..., :QK_NOPE_DIM].reshape(depth, KV_LORA_RANK, N_HEADS * QK_NOPE_DIM)
    v = w[..., QK_NOPE_DIM:].reshape(depth, KV_LORA_RANK, N_HEADS * V_HEAD_DIM)
    return jnp.concatenate([k_nope, v], axis=-1)


def kernel(x, positions, w_in, q_norm, w_q_up, kv_norm, w_kv_up, pool_w, pool_scale, w_out,
           norm_pre_mix, norm_post_mix, norm_pre_ffn, norm_post_ffn, w_gate, w_up, w_down):
    batch, seq, d_model = x.shape
    depth = w_in.shape[0]
    n = batch * seq

    tm_big = min(1024, n)
    tm_mid = min(512, n)
    rows_norm = min(256, n)
    rows_rope = min(2048, n)
    tq = min(2048, seq)
    tk = min(512, seq)
    ts = min(512, seq)
    assert all(n % t == 0 for t in (tm_big, tm_mid, rows_norm, rows_rope)), (batch, seq)
    assert seq % tq == 0 and tq % tk == 0 and seq % ts == 0 and ts % POOL_HALO == 0, seq

    inv_freq = ROPE_THETA ** (-jnp.arange(0, QK_ROPE_DIM, 2, dtype=F32) / QK_ROPE_DIM)
    invf = jnp.tile(inv_freq, LANES // inv_freq.shape[0])[None, :]
    half = QK_ROPE_DIM // 2
    sign = jnp.where((jnp.arange(LANES) % QK_ROPE_DIM) < half, -1.0, 1.0).astype(F32)[None, :]
    cos, sin = rope_tables(positions.reshape(n, 1), invf, sign, rows_rope)

    gain = lambda g, l: g[l][None, :]
    h = x.reshape(n, d_model)
    a = norm_cast(h, gain(norm_pre_mix, 0), rows_norm)
    w_in_b = _prep_w_in(w_in)
    w_q_b = _prep_w_q(w_q_up)
    w_kv_b = _prep_w_kv(w_kv_up)
    pool_w_b = pool_w.astype(BF16)
    w_out_b = w_out.astype(BF16)
    for l in range(depth):
        z = matmul(a, w_in_b, l, F32, tm_big, 512)
        qn, qr, kn, v, kr = qkv_project(z, cos, sin, gain(q_norm, l), gain(kv_norm, l),
                                        w_q_b, w_kv_b, l, tm_mid)
        attn, w_gate_b, w_up_b, w_down_b = causal_attention(qn, qr, kn, kr, v, w_gate, w_up, w_down, l,
                                                            batch, seq, tq, tk)
        pool = multiscale_pool(z, pool_w_b, gain(pool_scale, l), l, batch, seq, ts)
        h, a = outproj_resid(attn, pool, w_out_b, l, h, gain(norm_post_mix, l), gain(norm_pre_ffn, l),
                             tm_big, 256)
        g_next = gain(norm_pre_mix, l + 1) if l + 1 < depth else None
        h, a = swiglu_resid(a, w_gate_b, w_up_b, w_down_b, h, gain(norm_post_ffn, l), g_next, tm_big, 256)
    return h.reshape(batch, seq, d_model)
```

```python
import functools
import math

import jax
import jax.numpy as jnp
from jax import lax
from jax.experimental import pallas as pl
from jax.experimental.pallas import tpu as pltpu

F32 = jnp.float32
BF16 = jnp.bfloat16

N_HEADS = 16
QK_NOPE_DIM = 128
QK_ROPE_DIM = 64
V_HEAD_DIM = 128
Q_LORA_RANK = 896
KV_LORA_RANK = 512
POOL_WINDOWS = (2, 4, 8, 16)
POOL_GROUP_DIM = 512
POOL_WIDTH = POOL_GROUP_DIM * len(POOL_WINDOWS)
MLA_WIDTH = N_HEADS * V_HEAD_DIM
ROPE_PAIR_WIDTH = 2 * QK_ROPE_DIM
ATTN_SCALE = (QK_NOPE_DIM + QK_ROPE_DIM) ** -0.5
Q_PRESCALE = ATTN_SCALE * math.log2(math.e)
ROPE_THETA = 10000.0
NORM_EPS = 1e-6
MASK_VALUE = -1e30

LANES = 128
BF16_SUBLANES = 16
POOL_HALO = 16
VMEM_LIMIT_BYTES = 56 * 1024 * 1024
TWO_ACC_VMEM_LIMIT_BYTES = 60 * 1024 * 1024

Z_U = 0
Z_KV = POOL_WIDTH
Z_KR = Z_KV + KV_LORA_RANK
Z_Q = Z_KR + 2 * QK_ROPE_DIM
Z_WIDTH = Z_Q + Q_LORA_RANK


def _params(*sem):
    return pltpu.CompilerParams(dimension_semantics=sem, vmem_limit_bytes=VMEM_LIMIT_BYTES)


def _rms(x, g):
    ms = jnp.mean(x * x, axis=-1, keepdims=True)
    return x * lax.rsqrt(ms + NORM_EPS) * g


def _rope_table_kernel(pos_ref, invf_ref, sign_ref, cos_ref, sin_ref):
    ang = pos_ref[...].astype(F32) * invf_ref[...]
    cos_ref[...] = jnp.cos(ang)
    sin_ref[...] = jnp.sin(ang) * sign_ref[...]


def rope_tables(pos_col, invf, sign, rows):
    n = pos_col.shape[0]
    return pl.pallas_call(
        _rope_table_kernel,
        grid=(n // rows,),
        in_specs=[pl.BlockSpec((rows, 1), lambda i: (i, 0)),
                  pl.BlockSpec((1, LANES), lambda i: (0, 0)),
                  pl.BlockSpec((1, LANES), lambda i: (0, 0))],
        out_specs=[pl.BlockSpec((rows, LANES), lambda i: (i, 0))] * 2,
        out_shape=[jax.ShapeDtypeStruct((n, LANES), F32)] * 2,
        compiler_params=_params("parallel"),
        name="rope_tables",
    )(pos_col, invf, sign)


def _mm_kernel(a_ref, w_ref, o_ref):
    o_ref[...] = jnp.dot(a_ref[...], w_ref[...], preferred_element_type=F32).astype(o_ref.dtype)


def matmul(a, w, layer, out_dtype, tm, tn):
    m, k = a.shape
    n = w.shape[2]
    return pl.pallas_call(
        _mm_kernel,
        grid=(m // tm, n // tn),
        in_specs=[pl.BlockSpec((tm, k), lambda i, j: (i, 0)),
                  pl.BlockSpec((None, k, tn), lambda i, j: (layer, 0, j))],
        out_specs=pl.BlockSpec((tm, tn), lambda i, j: (i, j)),
        out_shape=jax.ShapeDtypeStruct((m, n), out_dtype),
        compiler_params=_params("parallel", "arbitrary"),
        name="matmul",
    )(a, w)


def _norm_matmul_kernel(x_ref, g_ref, w_ref, o_ref, a0, a1, *, nblocks, slabs, slab_rows):
    i = pl.program_id(0)
    j = pl.program_id(1)

    def produce(a_scr):
        @pl.when(j < slabs)
        def _():
            a_scr[j] = _rms(x_ref[...], g_ref[...]).astype(BF16)

    @pl.when(i == 0)
    def _():
        produce(a0)

    for parity, (a_cur, a_next) in enumerate(((a0, a1), (a1, a0))):
        @pl.when(jnp.logical_and(i > 0, (i - 1) % 2 == parity))
        def _(a_cur=a_cur, a_next=a_next):
            a = a_cur[...].reshape(slabs * slab_rows, a_cur.shape[2])
            o_ref[...] = jnp.dot(a, w_ref[...], preferred_element_type=F32)
            produce(a_next)


def norm_matmul(x, g, w, layer, tm, tn):
    m, k = x.shape
    n = w.shape[2]
    nblocks = m // tm
    nsteps = n // tn
    slab_rows = tm // 4
    slabs = tm // slab_rows
    assert slabs <= nsteps and slab_rows % BF16_SUBLANES == 0

    def x_index(i, j):
        return (jnp.minimum(i, nblocks - 1) * slabs + jnp.minimum(j, slabs - 1), 0)

    col = lambda i, j: jnp.where(i == 0, 0, j)
    return pl.pallas_call(
        functools.partial(_norm_matmul_kernel, nblocks=nblocks, slabs=slabs, slab_rows=slab_rows),
        grid=(nblocks + 1, nsteps),
        in_specs=[pl.BlockSpec((slab_rows, k), x_index),
                  pl.BlockSpec((1, k), lambda i, j: (0, 0)),
                  pl.BlockSpec((None, k, tn), lambda i, j: (layer, 0, col(i, j)))],
        out_specs=pl.BlockSpec((tm, tn), lambda i, j: (jnp.maximum(i - 1, 0), col(i, j))),
        out_shape=jax.ShapeDtypeStruct((m, n), F32),
        scratch_shapes=[pltpu.VMEM((slabs, slab_rows, k), BF16), pltpu.VMEM((slabs, slab_rows, k), BF16)],
        compiler_params=_params("arbitrary", "arbitrary"),
        name="norm_matmul",
    )(x, g, w)


def _resid_epilogue(read_rows, step, slabs, slab_rows, h_ref, gpost_ref, gnext_ref, ho_ref, ao_ref):
    slab = jnp.minimum(step, slabs - 1)
    rows = pl.ds(pl.multiple_of(slab * slab_rows, slab_rows), slab_rows)
    h = h_ref[...] + _rms(read_rows(rows), gpost_ref[...])
    ho_ref[...] = h
    if ao_ref is not None:
        ao_ref[...] = _rms(h, gnext_ref[...]).astype(BF16)


def _two_stream_step(i, nblocks, accs, matmuls, epilogue, prepare=None):
    for parity in (0, 1):
        @pl.when(jnp.logical_and(i < nblocks, i % 2 == parity))
        def _(parity=parity):
            if prepare is not None:
                prepare(accs[parity])
            epilogue(accs[1 - parity])
            matmuls(accs[parity])

    @pl.when(i == nblocks)
    def _():
        epilogue(accs[(nblocks - 1) % 2])


def _lagged_slab_index(slabs):
    def index(i, j):
        return (jnp.maximum((i - 1) * slabs + jnp.minimum(j, slabs - 1), 0), 0)
    return index


def _outproj_resid_kernel(a1_ref, a2_ref, w1_ref, w2_ref, h_ref, gpost_ref, gnext_ref,
                          ho_ref, ao_ref, acc0, acc1, *, nblocks, slabs, slab_rows):
    i = pl.program_id(0)
    j = pl.program_id(1)
    nsteps = acc0.shape[0]

    @pl.when(jnp.logical_and(i == 0, j == 0))
    def _():
        acc1[...] = jnp.zeros(acc1.shape, F32)

    def matmuls(acc):
        acc[j] = (jnp.dot(a1_ref[...], w1_ref[...], preferred_element_type=F32)
                  + jnp.dot(a2_ref[...], w2_ref[...], preferred_element_type=F32))

    def epilogue(acc):
        read_rows = lambda rows: jnp.concatenate([acc[c, rows, :] for c in range(nsteps)], axis=1)
        _resid_epilogue(read_rows, j, slabs, slab_rows, h_ref, gpost_ref, gnext_ref, ho_ref, ao_ref)

    _two_stream_step(i, nblocks, (acc0, acc1), matmuls, epilogue)


def outproj_resid(a1, a2, w, h, g_post, g_next, tm, tn):
    m, k1 = a1.shape
    k2 = a2.shape[1]
    d = w.shape[1]
    assert k1 == k2 and w.shape[0] == k1 + k2
    nblocks = m // tm
    nsteps = d // tn
    slabs = nsteps
    slab_rows = tm // slabs
    assert tm % slabs == 0 and slab_rows % BF16_SUBLANES == 0

    block_row = lambda i, j: (jnp.minimum(i, nblocks - 1), 0)
    w_col = lambda i, j: jnp.where(i < nblocks, j, nsteps - 1)
    gain = pl.BlockSpec((1, d), lambda i, j: (0, 0))
    slab_spec = pl.BlockSpec((slab_rows, d), _lagged_slab_index(slabs))
    return pl.pallas_call(
        functools.partial(_outproj_resid_kernel, nblocks=nblocks, slabs=slabs, slab_rows=slab_rows),
        grid=(nblocks + 1, nsteps),
        in_specs=[pl.BlockSpec((tm, k1), block_row),
                  pl.BlockSpec((tm, k2), block_row),
                  pl.BlockSpec((k1, tn), lambda i, j: (0, w_col(i, j))),
                  pl.BlockSpec((k2, tn), lambda i, j: (1, w_col(i, j))),
                  slab_spec, gain, gain],
        out_specs=[slab_spec, slab_spec],
        out_shape=[jax.ShapeDtypeStruct((m, d), F32), jax.ShapeDtypeStruct((m, d), BF16)],
        scratch_shapes=[pltpu.VMEM((nsteps, tm, tn), F32), pltpu.VMEM((nsteps, tm, tn), F32)],
        compiler_params=pltpu.CompilerParams(dimension_semantics=("arbitrary", "arbitrary"),
                                             vmem_limit_bytes=TWO_ACC_VMEM_LIMIT_BYTES),
        name="outproj_resid",
    )(a1, a2, w, w, h, g_post, g_next)


def _qkv_kernel(cq_ref, ckv_ref, kr_ref, cos_ref, sin_ref, qg_ref, kvg_ref, wq_ref, wkv_ref,
                qn_ref, qr_ref, kn_ref, v_ref, kro_ref):
    cos = cos_ref[...]
    sin = sin_ref[...]
    tm = cos.shape[0]
    lane = lax.broadcasted_iota(jnp.int32, (tm, LANES), 1)
    first_half = (lane % QK_ROPE_DIM) < (QK_ROPE_DIM // 2)
    chunk = 512

    cqn = _rms(cq_ref[...], qg_ref[...]).astype(BF16)
    for c in range(MLA_WIDTH // chunk):
        sl = slice(c * chunk, (c + 1) * chunk)
        qn = jnp.dot(cqn, wq_ref[:, sl], preferred_element_type=F32)
        qn_ref[:, sl] = (qn * Q_PRESCALE).astype(BF16)
    t_all = jnp.dot(cqn, wq_ref[:, MLA_WIDTH:], preferred_element_type=F32)
    for c in range(t_all.shape[1] // LANES):
        t = t_all[:, c * LANES:(c + 1) * LANES]
        swapped = jnp.where(first_half, pltpu.roll(t, LANES - 32, 1), pltpu.roll(t, 32, 1))
        qr_ref[:, c * LANES:(c + 1) * LANES] = ((t * cos + swapped * sin) * Q_PRESCALE).astype(BF16)

    ckvn = _rms(ckv_ref[...], kvg_ref[...]).astype(BF16)
    for c in range(MLA_WIDTH // chunk):
        sl = slice(c * chunk, (c + 1) * chunk)
        sl_v = slice(MLA_WIDTH + c * chunk, MLA_WIDTH + (c + 1) * chunk)
        kn_ref[:, sl] = jnp.dot(ckvn, wkv_ref[:, sl], preferred_element_type=F32).astype(BF16)
        v_ref[:, sl] = jnp.dot(ckvn, wkv_ref[:, sl_v], preferred_element_type=F32).astype(BF16)

    kr = kr_ref[...]
    kr_rot = kr * cos + pltpu.roll(kr, 32, 1) * sin
    zero = jnp.zeros_like(kr_rot)
    kro_ref[:, :LANES] = jnp.where(lane < QK_ROPE_DIM, kr_rot, zero).astype(BF16)
    kro_ref[:, LANES:] = jnp.where(lane >= QK_ROPE_DIM, kr_rot, zero).astype(BF16)


def qkv_project(z, cos, sin, q_gain, kv_gain, wq, wkv, layer, tm):
    n = z.shape[0]
    rope_w = N_HEADS * QK_ROPE_DIM
    row = lambda i: (i, 0)
    const = lambda i: (0, 0)
    weight = lambda i: (layer, 0, 0)
    return pl.pallas_call(
        _qkv_kernel,
        grid=(n // tm,),
        in_specs=[pl.BlockSpec((tm, Q_LORA_RANK), lambda i: (i, Z_Q // Q_LORA_RANK)),
                  pl.BlockSpec((tm, KV_LORA_RANK), lambda i: (i, Z_KV // KV_LORA_RANK)),
                  pl.BlockSpec((tm, LANES), lambda i: (i, Z_KR // LANES)),
                  pl.BlockSpec((tm, LANES), row),
                  pl.BlockSpec((tm, LANES), row),
                  pl.BlockSpec((1, Q_LORA_RANK), const),
                  pl.BlockSpec((1, KV_LORA_RANK), const),
                  pl.BlockSpec((None,) + wq.shape[1:], weight),
                  pl.BlockSpec((None,) + wkv.shape[1:], weight)],
        out_specs=[pl.BlockSpec((tm, MLA_WIDTH), row),
                   pl.BlockSpec((tm, rope_w), row),
                   pl.BlockSpec((tm, MLA_WIDTH), row),
                   pl.BlockSpec((tm, MLA_WIDTH), row),
                   pl.BlockSpec((tm, 2 * LANES), row)],
        out_shape=[jax.ShapeDtypeStruct((n, MLA_WIDTH), BF16),
                   jax.ShapeDtypeStruct((n, rope_w), BF16),
                   jax.ShapeDtypeStruct((n, MLA_WIDTH), BF16),
                   jax.ShapeDtypeStruct((n, MLA_WIDTH), BF16),
                   jax.ShapeDtypeStruct((n, 2 * LANES), BF16)],
        compiler_params=_params("parallel"),
        name="qkv_project",
    )(z, z, z, cos, sin, q_gain, kv_gain, wq, wkv)


def _scores(q, k):
    return lax.dot_general(q, k, (((1,), (1,)), ((), ())), preferred_element_type=F32)


def _online_softmax_update(s, v, state, mask):
    m_prev, l_prev, acc_prev = state
    if mask is not None:
        top = mask.shape[0]
        s_top = jnp.where(mask, s[:top], MASK_VALUE)
        s = s_top if s.shape[0] == top else jnp.concatenate([s_top, s[top:]], axis=0)
    m_next = jnp.maximum(m_prev, jnp.max(s, axis=1, keepdims=True))
    alpha = jnp.exp2(m_prev - m_next)
    ps = [jnp.exp2(s[:, c * LANES:(c + 1) * LANES] - m_next) for c in range(s.shape[1] // LANES)]
    l_next = alpha * l_prev + functools.reduce(lambda a, b: a + b, ps)
    p = jnp.concatenate(ps, axis=1).astype(BF16)
    acc_next = alpha * acc_prev + jnp.dot(p, v, preferred_element_type=F32)
    return m_next, l_next, acc_next


def _attn_kernel(qn_ref, qr_ref, kn_ref, kr_ref, v_ref, *rest, tq, tk, n_cast):
    w_refs = rest[:n_cast]
    o_ref = rest[n_cast]
    wo_refs = rest[n_cast + 1:2 * n_cast + 1]
    m_scr, l_scr, acc_scr = rest[2 * n_cast + 1:]
    for w_ref, wo_ref in zip(w_refs, wo_refs):
        wo_ref[...] = w_ref[...].astype(BF16)

    i = pl.program_id(2)
    per_tile = tq // tk
    m_scr[...] = jnp.full(m_scr.shape, MASK_VALUE, F32)
    l_scr[...] = jnp.zeros(l_scr.shape, F32)
    acc_scr[...] = jnp.zeros(acc_scr.shape, F32)

    def load_q(rows):
        return jnp.concatenate([qn_ref[rows, :], qr_ref[rows, :]], axis=1)

    def load_kv(start):
        k = jnp.concatenate([kn_ref[pl.ds(start, tk), :], kr_ref[pl.ds(start, tk), :]], axis=1)
        return k, v_ref[pl.ds(start, tk), :]

    def load_state(rows):
        return m_scr[rows, :], l_scr[rows, :], acc_scr[rows, :]

    def store_state(rows, state):
        m_scr[rows, :], l_scr[rows, :], acc_scr[rows, :] = state

    everything = slice(0, tq)

    def body(g, carry):
        q = load_q(everything)
        state = load_state(everything)
        for u in range(per_tile):
            k, v = load_kv(pl.multiple_of((g * per_tile + u) * tk, tk))
            state = _online_softmax_update(_scores(q, k), v, state, None)
        store_state(everything, state)
        return carry

    lax.fori_loop(0, i, body, 0)

    for d in range(per_tile):
        rows = slice(d * tk, tq)
        k, v = load_kv(pl.multiple_of(i * tq + d * tk, tk))
        row = lax.broadcasted_iota(jnp.int32, (tk, tk), 0)
        col = lax.broadcasted_iota(jnp.int32, (tk, tk), 1)
        state = _online_softmax_update(_scores(load_q(rows), k), v, load_state(rows), col <= row)
        store_state(rows, state)

    l = jnp.sum(l_scr[...], axis=1, keepdims=True)
    o_ref[...] = (acc_scr[...] / l).astype(o_ref.dtype)


def _slab_rows(total_rows, steps):
    rows = BF16_SUBLANES
    while total_rows % rows or total_rows // rows > steps:
        rows += BF16_SUBLANES
    return rows


def causal_attention(qn, qr, kn, kr, v, cast_weights, layer, batch, seq, tq, tk):
    n = qn.shape[0]
    nq = seq // tq
    steps = batch * N_HEADS * nq

    w_in_specs, w_out_specs, w_out_shapes = [], [], []
    for w in cast_weights:
        _, nrows, ncols = w.shape
        rows = _slab_rows(nrows, steps)
        last = nrows // rows - 1
        slab = lambda b, h, i, last=last: (jnp.minimum((b * N_HEADS + h) * nq + i, last), 0)
        w_in_specs.append(pl.BlockSpec((None, rows, ncols), lambda b, h, i, slab=slab: (layer,) + slab(b, h, i)))
        w_out_specs.append(pl.BlockSpec((rows, ncols), slab))
        w_out_shapes.append(jax.ShapeDtypeStruct((nrows, ncols), BF16))

    return pl.pallas_call(
        functools.partial(_attn_kernel, tq=tq, tk=tk, n_cast=len(cast_weights)),
        grid=(batch, N_HEADS, nq),
        in_specs=[pl.BlockSpec((tq, QK_NOPE_DIM), lambda b, h, i: (b * nq + i, h)),
                  pl.BlockSpec((tq, ROPE_PAIR_WIDTH), lambda b, h, i: (b * nq + i, h // 2)),
                  pl.BlockSpec((seq, QK_NOPE_DIM), lambda b, h, i: (b, h)),
                  pl.BlockSpec((seq, ROPE_PAIR_WIDTH), lambda b, h, i: (b, h % 2)),
                  pl.BlockSpec((seq, V_HEAD_DIM), lambda b, h, i: (b, h))] + w_in_specs,
        out_specs=[pl.BlockSpec((tq, V_HEAD_DIM), lambda b, h, i: (b * nq + i, h))] + w_out_specs,
        out_shape=[jax.ShapeDtypeStruct((n, MLA_WIDTH), BF16)] + w_out_shapes,
        scratch_shapes=[pltpu.VMEM((tq, LANES), F32),
                        pltpu.VMEM((tq, LANES), F32),
                        pltpu.VMEM((tq, V_HEAD_DIM), F32)],
        compiler_params=_params("arbitrary", "arbitrary", "arbitrary"),
        name="causal_attention",
    )(qn, qr, kn, kr, v, *cast_weights)


def _pool_kernel(u_ref, halo_ref, w_ref, scale_ref, o_ref, *, ts):
    i = pl.program_id(1)
    t = i * ts + lax.broadcasted_iota(jnp.int32, (ts, 1), 0)
    for g, window in enumerate(POOL_WINDOWS):
        sl = slice(g * POOL_GROUP_DIM, (g + 1) * POOL_GROUP_DIM)
        x = u_ref[:, sl]
        halo = jnp.where(i > 0, halo_ref[:, sl], 0.0)
        s = jnp.concatenate([halo, x], axis=0)
        shift = 1
        while shift < window:
            s = s + pltpu.roll(s, shift, 0)
            shift *= 2
        cnt = jnp.minimum(t + 1, window).astype(F32)
        d = s[POOL_HALO:, :] / cnt - x
        y = jnp.dot(d.astype(BF16), w_ref[g], preferred_element_type=F32)
        o_ref[:, sl] = (y * scale_ref[:, sl]).astype(o_ref.dtype)


def multiscale_pool(z, pool_w, pool_scale, layer, batch, seq, ts):
    n = z.shape[0]
    ns = seq // ts
    halo_blocks = ts // POOL_HALO
    return pl.pallas_call(
        functools.partial(_pool_kernel, ts=ts),
        grid=(batch, ns),
        in_specs=[pl.BlockSpec((ts, POOL_WIDTH), lambda b, i: (b * ns + i, Z_U // POOL_WIDTH)),
                  pl.BlockSpec((POOL_HALO, POOL_WIDTH),
                               lambda b, i: (jnp.maximum((b * ns + i) * halo_blocks - 1, 0), 0)),
                  pl.BlockSpec((None,) + pool_w.shape[1:], lambda b, i: (layer, 0, 0, 0)),
                  pl.BlockSpec((1, POOL_WIDTH), lambda b, i: (0, 0))],
        out_specs=pl.BlockSpec((ts, POOL_WIDTH), lambda b, i: (b * ns + i, 0)),
        out_shape=jax.ShapeDtypeStruct((n, POOL_WIDTH), BF16),
        compiler_params=_params("parallel", "parallel"),
        name="multiscale_pool",
    )(z, z, pool_w, pool_scale)


def _ffn_resid_kernel(a_ref, wg_ref, wu_ref, wd_ref, h_ref, gpost_ref, *rest,
                      chunk, nblocks, slabs, slab_rows, with_next):
    if with_next:
        gnext_ref, ho_ref, ao_ref, acc0, acc1 = rest
    else:
        ho_ref, acc0, acc1 = rest
        gnext_ref = ao_ref = None
    i = pl.program_id(0)
    f = pl.program_id(1)

    @pl.when(jnp.logical_and(i == 0, f == 0))
    def _():
        acc1[...] = jnp.zeros(acc1.shape, F32)

    def prepare(acc):
        @pl.when(f == 0)
        def _():
            acc[...] = jnp.zeros(acc.shape, F32)

    def matmuls(acc):
        a = a_ref[...]
        g = jnp.dot(a, wg_ref[...], preferred_element_type=F32)
        u = jnp.dot(a, wu_ref[...], preferred_element_type=F32)
        p = (g * (1.0 / (1.0 + jnp.exp(-g))) * u).astype(BF16)
        for c in range(acc.shape[1] // chunk):
            sl = slice(c * chunk, (c + 1) * chunk)
            acc[:, sl] += jnp.dot(p, wd_ref[:, sl], preferred_element_type=F32)

    def epilogue(acc):
        _resid_epilogue(lambda rows: acc[rows, :], f, slabs, slab_rows,
                        h_ref, gpost_ref, gnext_ref, ho_ref, ao_ref)

    _two_stream_step(i, nblocks, (acc0, acc1), matmuls, epilogue, prepare)


def swiglu_resid(a, w_gate, w_up, w_down, h, g_post, g_next, tm, tf):
    n, d = a.shape
    d_ff = w_gate.shape[1]
    nblocks = n // tm
    nf = d_ff // tf
    slab_rows = 2 * BF16_SUBLANES
    slabs = tm // slab_rows
    assert slabs <= nf and tm % slab_rows == 0
    with_next = g_next is not None

    def f_index(i, f):
        return jnp.where(i < nblocks, f, nf - 1)

    gain = pl.BlockSpec((1, d), lambda i, f: (0, 0))
    slab_spec = pl.BlockSpec((slab_rows, d), _lagged_slab_index(slabs))
    in_specs = [pl.BlockSpec((tm, d), lambda i, f: (jnp.minimum(i, nblocks - 1), 0),
                             pipeline_mode=pl.Buffered(1)),
                pl.BlockSpec((d, tf), lambda i, f: (0, f_index(i, f))),
                pl.BlockSpec((d, tf), lambda i, f: (0, f_index(i, f))),
                pl.BlockSpec((tf, d), lambda i, f: (f_index(i, f), 0)),
                slab_spec, gain]
    args = [a, w_gate, w_up, w_down, h, g_post]
    out_specs = [slab_spec]
    out_shape = [jax.ShapeDtypeStruct((n, d), F32)]
    if with_next:
        in_specs.append(gain)
        args.append(g_next)
        out_specs.append(slab_spec)
        out_shape.append(jax.ShapeDtypeStruct((n, d), BF16))
    outs = pl.pallas_call(
        functools.partial(_ffn_resid_kernel, chunk=512, nblocks=nblocks, slabs=slabs,
                          slab_rows=slab_rows, with_next=with_next),
        grid=(nblocks + 1, nf),
        in_specs=in_specs,
        out_specs=out_specs,
        out_shape=out_shape,
        scratch_shapes=[pltpu.VMEM((tm, d), F32), pltpu.VMEM((tm, d), F32)],
        compiler_params=pltpu.CompilerParams(dimension_semantics=("arbitrary", "arbitrary"),
                                             vmem_limit_bytes=TWO_ACC_VMEM_LIMIT_BYTES),
        name="swiglu_resid",
    )(*args)
    return (outs[0], outs[1]) if with_next else (outs[0], None)


def _prep_w_in(w_in):
    o1 = Q_LORA_RANK
    o2 = o1 + KV_LORA_RANK
    o3 = o2 + QK_ROPE_DIM
    w = w_in.astype(BF16)
    kr = w[..., o2:o3]
    return jnp.concatenate([w[..., o3:], w[..., o1:o2], kr, kr, w[..., :o1]], axis=-1)


def _prep_w_q(w_q_up):
    depth = w_q_up.shape[0]
    w = w_q_up.astype(BF16).reshape(depth, Q_LORA_RANK, N_HEADS, QK_NOPE_DIM + QK_ROPE_DIM)
    nope = w[..., :QK_NOPE_DIM].reshape(depth, Q_LORA_RANK, N_HEADS * QK_NOPE_DIM)
    rope = w[..., QK_NOPE_DIM:].reshape(depth, Q_LORA_RANK, N_HEADS * QK_ROPE_DIM)
    return jnp.concatenate([nope, rope], axis=-1)


def _prep_w_kv(w_kv_up):
    depth = w_kv_up.shape[0]
    w = w_kv_up.astype(BF16).reshape(depth, KV_LORA_RANK, N_HEADS, QK_NOPE_DIM + V_HEAD_DIM)
    k_nope = w[..., :QK_NOPE_DIM].reshape(depth, KV_LORA_RANK, N_HEADS * QK_NOPE_DIM)
    v = w[..., QK_NOPE_DIM:].reshape(depth, KV_LORA_RANK, N_HEADS * V_HEAD_DIM)
    return jnp.concatenate([k_nope, v], axis=-1)


def kernel(x, positions, w_in, q_norm, w_q_up, kv_norm, w_kv_up, pool_w, pool_scale, w_out,
           norm_pre_mix, norm_post_mix, norm_pre_ffn, norm_post_ffn, w_gate, w_up, w_down):
    batch, seq, d_model = x.shape
    depth = w_in.shape[0]
    n = batch * seq

    tm_big = min(1024, n)
    tm_mid = min(512, n)
    rows_rope = min(2048, n)
    tq = min(2048, seq)
    tk = min(512, seq)
    ts = min(512, seq)
    assert all(n % t == 0 for t in (tm_big, tm_mid, rows_rope)), (batch, seq)
    assert seq % tq == 0 and tq % tk == 0 and seq % ts == 0 and ts % POOL_HALO == 0, seq

    inv_freq = ROPE_THETA ** (-jnp.arange(0, QK_ROPE_DIM, 2, dtype=F32) / QK_ROPE_DIM)
    invf = jnp.tile(inv_freq, LANES // inv_freq.shape[0])[None, :]
    half = QK_ROPE_DIM // 2
    sign = jnp.where((jnp.arange(LANES) % QK_ROPE_DIM) < half, -1.0, 1.0).astype(F32)[None, :]
    cos, sin = rope_tables(positions.reshape(n, 1), invf, sign, rows_rope)

    gain = lambda g, l: g[l][None, :]
    h = x.reshape(n, d_model)
    a = None
    w_in_b = _prep_w_in(w_in)
    w_q_b = _prep_w_q(w_q_up)
    w_kv_b = _prep_w_kv(w_kv_up)
    pool_w_b = pool_w.astype(BF16)
    for l in range(depth):
        if a is None:
            z = norm_matmul(h, gain(norm_pre_mix, l), w_in_b, l, tm_big, 512)
        else:
            z = matmul(a, w_in_b, l, F32, tm_big, 512)
        qn, qr, kn, v, kr = qkv_project(z, cos, sin, gain(q_norm, l), gain(kv_norm, l),
                                        w_q_b, w_kv_b, l, tm_mid)
        attn, w_out_b, w_gate_b, w_up_b, w_down_b = causal_attention(
            qn, qr, kn, kr, v, (w_out, w_gate, w_up, w_down), l, batch, seq, tq, tk)
        pool = multiscale_pool(z, pool_w_b, gain(pool_scale, l), l, batch, seq, ts)
        h, a = outproj_resid(attn, pool, w_out_b, h, gain(norm_post_mix, l), gain(norm_pre_ffn, l),
                             tm_big, 256)
        g_next = gain(norm_pre_mix, l + 1) if l + 1 < depth else None
        h, a = swiglu_resid(a, w_gate_b, w_up_b, w_down_b, h, gain(norm_post_ffn, l), g_next, tm_big, 256)
    return h.reshape(batch, seq, d_model)
```

```python
import functools
import math

import jax
import jax.numpy as jnp
from jax import lax
from jax.experimental import pallas as pl
from jax.experimental.pallas import tpu as pltpu

F32 = jnp.float32
BF16 = jnp.bfloat16

N_HEADS = 16
QK_NOPE_DIM = 128
QK_ROPE_DIM = 64
V_HEAD_DIM = 128
Q_LORA_RANK = 896
KV_LORA_RANK = 512
POOL_WINDOWS = (2, 4, 8, 16)
POOL_GROUP_DIM = 512
POOL_WIDTH = POOL_GROUP_DIM * len(POOL_WINDOWS)
MLA_WIDTH = N_HEADS * V_HEAD_DIM
ROPE_PAIR_WIDTH = 2 * QK_ROPE_DIM
ATTN_SCALE = (QK_NOPE_DIM + QK_ROPE_DIM) ** -0.5
Q_PRESCALE = ATTN_SCALE * math.log2(math.e)
ROPE_THETA = 10000.0
NORM_EPS = 1e-6
MASK_VALUE = -1e30

LANES = 128
BF16_SUBLANES = 16
POOL_HALO = 16
VMEM_LIMIT_BYTES = 56 * 1024 * 1024
TWO_ACC_VMEM_LIMIT_BYTES = 60 * 1024 * 1024

Z_U = 0
Z_KV = POOL_WIDTH
Z_KR = Z_KV + KV_LORA_RANK
Z_Q = Z_KR + 2 * QK_ROPE_DIM
Z_WIDTH = Z_Q + Q_LORA_RANK


def _params(*sem):
    return pltpu.CompilerParams(dimension_semantics=sem, vmem_limit_bytes=VMEM_LIMIT_BYTES)


def _rms(x, g):
    ms = jnp.mean(x * x, axis=-1, keepdims=True)
    return x * lax.rsqrt(ms + NORM_EPS) * g


def _rope_table_kernel(pos_ref, invf_ref, sign_ref, cos_ref, sin_ref):
    ang = pos_ref[...].astype(F32) * invf_ref[...]
    cos_ref[...] = jnp.cos(ang)
    sin_ref[...] = jnp.sin(ang) * sign_ref[...]


def rope_tables(pos_col, invf, sign, rows):
    n = pos_col.shape[0]
    return pl.pallas_call(
        _rope_table_kernel,
        grid=(n // rows,),
        in_specs=[pl.BlockSpec((rows, 1), lambda i: (i, 0)),
                  pl.BlockSpec((1, LANES), lambda i: (0, 0)),
                  pl.BlockSpec((1, LANES), lambda i: (0, 0))],
        out_specs=[pl.BlockSpec((rows, LANES), lambda i: (i, 0))] * 2,
        out_shape=[jax.ShapeDtypeStruct((n, LANES), F32)] * 2,
        compiler_params=_params("parallel"),
        name="rope_tables",
    )(pos_col, invf, sign)


def _mm_kernel(a_ref, w_ref, o_ref):
    o_ref[...] = jnp.dot(a_ref[...], w_ref[...], preferred_element_type=F32).astype(o_ref.dtype)


def matmul(a, w, layer, out_dtype, tm, tn):
    m, k = a.shape
    n = w.shape[2]
    return pl.pallas_call(
        _mm_kernel,
        grid=(m // tm, n // tn),
        in_specs=[pl.BlockSpec((tm, k), lambda i, j: (i, 0)),
                  pl.BlockSpec((None, k, tn), lambda i, j: (layer, 0, j))],
        out_specs=pl.BlockSpec((tm, tn), lambda i, j: (i, j)),
        out_shape=jax.ShapeDtypeStruct((m, n), out_dtype),
        compiler_params=_params("parallel", "arbitrary"),
        name="matmul",
    )(a, w)


def _norm_matmul_kernel(x_ref, g_ref, w_ref, o_ref, a0, a1, *, nblocks, slabs, slab_rows):
    i = pl.program_id(0)
    j = pl.program_id(1)

    def produce(a_scr):
        @pl.when(j < slabs)
        def _():
            a_scr[j] = _rms(x_ref[...], g_ref[...]).astype(BF16)

    @pl.when(i == 0)
    def _():
        produce(a0)

    for parity, (a_cur, a_next) in enumerate(((a0, a1), (a1, a0))):
        @pl.when(jnp.logical_and(i > 0, (i - 1) % 2 == parity))
        def _(a_cur=a_cur, a_next=a_next):
            a = a_cur[...].reshape(slabs * slab_rows, a_cur.shape[2])
            o_ref[...] = jnp.dot(a, w_ref[...], preferred_element_type=F32)
            produce(a_next)


def norm_matmul(x, g, w, layer, tm, tn):
    m, k = x.shape
    n = w.shape[2]
    nblocks = m // tm
    nsteps = n // tn
    slab_rows = tm // 4
    slabs = tm // slab_rows
    assert slabs <= nsteps and slab_rows % BF16_SUBLANES == 0

    def x_index(i, j):
        return (jnp.minimum(i, nblocks - 1) * slabs + jnp.minimum(j, slabs - 1), 0)

    col = lambda i, j: jnp.where(i == 0, 0, j)
    return pl.pallas_call(
        functools.partial(_norm_matmul_kernel, nblocks=nblocks, slabs=slabs, slab_rows=slab_rows),
        grid=(nblocks + 1, nsteps),
        in_specs=[pl.BlockSpec((slab_rows, k), x_index),
                  pl.BlockSpec((1, k), lambda i, j: (0, 0)),
                  pl.BlockSpec((None, k, tn), lambda i, j: (layer, 0, col(i, j)))],
        out_specs=pl.BlockSpec((tm, tn), lambda i, j: (jnp.maximum(i - 1, 0), col(i, j))),
        out_shape=jax.ShapeDtypeStruct((m, n), F32),
        scratch_shapes=[pltpu.VMEM((slabs, slab_rows, k), BF16), pltpu.VMEM((slabs, slab_rows, k), BF16)],
        compiler_params=_params("arbitrary", "arbitrary"),
        name="norm_matmul",
    )(x, g, w)


def _resid_epilogue(read_rows, step, slabs, slab_rows, h_ref, gpost_ref, gnext_ref, ho_ref, ao_ref):
    slab = jnp.minimum(step, slabs - 1)
    rows = pl.ds(pl.multiple_of(slab * slab_rows, slab_rows), slab_rows)
    h = h_ref[...] + _rms(read_rows(rows), gpost_ref[...])
    ho_ref[...] = h
    if ao_ref is not None:
        ao_ref[...] = _rms(h, gnext_ref[...]).astype(BF16)


def _two_stream_step(i, nblocks, accs, matmuls, epilogue, prepare=None):
    for parity in (0, 1):
        @pl.when(jnp.logical_and(i < nblocks, i % 2 == parity))
        def _(parity=parity):
            if prepare is not None:
                prepare(accs[parity])
            epilogue(accs[1 - parity])
            matmuls(accs[parity])

    @pl.when(i == nblocks)
    def _():
        epilogue(accs[(nblocks - 1) % 2])


def _lagged_slab_index(slabs):
    def index(i, j):
        return (jnp.maximum((i - 1) * slabs + jnp.minimum(j, slabs - 1), 0), 0)
    return index


def _outproj_resid_kernel(a1_ref, a2_ref, w1_ref, w2_ref, h_ref, gpost_ref, gnext_ref,
                          ho_ref, ao_ref, acc0, acc1, *, nblocks, slabs, slab_rows):
    i = pl.program_id(0)
    j = pl.program_id(1)
    nsteps = acc0.shape[0]

    @pl.when(jnp.logical_and(i == 0, j == 0))
    def _():
        acc1[...] = jnp.zeros(acc1.shape, F32)

    def matmuls(acc):
        acc[j] = (jnp.dot(a1_ref[...], w1_ref[...], preferred_element_type=F32)
                  + jnp.dot(a2_ref[...], w2_ref[...], preferred_element_type=F32))

    def epilogue(acc):
        read_rows = lambda rows: jnp.concatenate([acc[c, rows, :] for c in range(nsteps)], axis=1)
        _resid_epilogue(read_rows, j, slabs, slab_rows, h_ref, gpost_ref, gnext_ref, ho_ref, ao_ref)

    _two_stream_step(i, nblocks, (acc0, acc1), matmuls, epilogue)


def outproj_resid(a1, a2, w, h, g_post, g_next, tm, tn):
    m, k1 = a1.shape
    k2 = a2.shape[1]
    d = w.shape[1]
    assert k1 == k2 and w.shape[0] == k1 + k2
    nblocks = m // tm
    nsteps = d // tn
    slabs = nsteps
    slab_rows = tm // slabs
    assert tm % slabs == 0 and slab_rows % BF16_SUBLANES == 0

    block_row = lambda i, j: (jnp.minimum(i, nblocks - 1), 0)
    w_col = lambda i, j: jnp.where(i < nblocks, j, nsteps - 1)
    gain = pl.BlockSpec((1, d), lambda i, j: (0, 0))
    slab_spec = pl.BlockSpec((slab_rows, d), _lagged_slab_index(slabs))
    return pl.pallas_call(
        functools.partial(_outproj_resid_kernel, nblocks=nblocks, slabs=slabs, slab_rows=slab_rows),
        grid=(nblocks + 1, nsteps),
        in_specs=[pl.BlockSpec((tm, k1), block_row),
                  pl.BlockSpec((tm, k2), block_row),
                  pl.BlockSpec((k1, tn), lambda i, j: (0, w_col(i, j))),
                  pl.BlockSpec((k2, tn), lambda i, j: (1, w_col(i, j))),
                  slab_spec, gain, gain],
        out_specs=[slab_spec, slab_spec],
        out_shape=[jax.ShapeDtypeStruct((m, d), F32), jax.ShapeDtypeStruct((m, d), BF16)],
        scratch_shapes=[pltpu.VMEM((nsteps, tm, tn), F32), pltpu.VMEM((nsteps, tm, tn), F32)],
        compiler_params=pltpu.CompilerParams(dimension_semantics=("arbitrary", "arbitrary"),
                                             vmem_limit_bytes=TWO_ACC_VMEM_LIMIT_BYTES),
        name="outproj_resid",
    )(a1, a2, w, w, h, g_post, g_next)


def _qkv_kernel(cq_ref, ckv_ref, kr_ref, cos_ref, sin_ref, qg_ref, kvg_ref, wq_ref, wkv_ref,
                qn_ref, qr_ref, kn_ref, v_ref, kro_ref):
    cos = cos_ref[...]
    sin = sin_ref[...]
    tm = cos.shape[0]
    lane = lax.broadcasted_iota(jnp.int32, (tm, LANES), 1)
    first_half = (lane % QK_ROPE_DIM) < (QK_ROPE_DIM // 2)
    chunk = 512

    cqn = _rms(cq_ref[...], qg_ref[...]).astype(BF16)
    for c in range(MLA_WIDTH // chunk):
        sl = slice(c * chunk, (c + 1) * chunk)
        qn = jnp.dot(cqn, wq_ref[:, sl], preferred_element_type=F32)
        qn_ref[:, sl] = (qn * Q_PRESCALE).astype(BF16)
    t_all = jnp.dot(cqn, wq_ref[:, MLA_WIDTH:], preferred_element_type=F32)
    for c in range(t_all.shape[1] // LANES):
        t = t_all[:, c * LANES:(c + 1) * LANES]
        swapped = jnp.where(first_half, pltpu.roll(t, LANES - 32, 1), pltpu.roll(t, 32, 1))
        qr_ref[:, c * LANES:(c + 1) * LANES] = ((t * cos + swapped * sin) * Q_PRESCALE).astype(BF16)

    ckvn = _rms(ckv_ref[...], kvg_ref[...]).astype(BF16)
    for c in range(MLA_WIDTH // chunk):
        sl = slice(c * chunk, (c + 1) * chunk)
        sl_v = slice(MLA_WIDTH + c * chunk, MLA_WIDTH + (c + 1) * chunk)
        kn_ref[:, sl] = jnp.dot(ckvn, wkv_ref[:, sl], preferred_element_type=F32).astype(BF16)
        v_ref[:, sl] = jnp.dot(ckvn, wkv_ref[:, sl_v], preferred_element_type=F32).astype(BF16)

    kr = kr_ref[...]
    kr_rot = kr * cos + pltpu.roll(kr, 32, 1) * sin
    zero = jnp.zeros_like(kr_rot)
    kro_ref[:, :LANES] = jnp.where(lane < QK_ROPE_DIM, kr_rot, zero).astype(BF16)
    kro_ref[:, LANES:] = jnp.where(lane >= QK_ROPE_DIM, kr_rot, zero).astype(BF16)


def qkv_project(z, cos, sin, q_gain, kv_gain, wq, wkv, layer, tm):
    n = z.shape[0]
    rope_w = N_HEADS * QK_ROPE_DIM
    row = lambda i: (i, 0)
    const = lambda i: (0, 0)
    weight = lambda i: (layer, 0, 0)
    return pl.pallas_call(
        _qkv_kernel,
        grid=(n // tm,),
        in_specs=[pl.BlockSpec((tm, Q_LORA_RANK), lambda i: (i, Z_Q // Q_LORA_RANK)),
                  pl.BlockSpec((tm, KV_LORA_RANK), lambda i: (i, Z_KV // KV_LORA_RANK)),
                  pl.BlockSpec((tm, LANES), lambda i: (i, Z_KR // LANES)),
                  pl.BlockSpec((tm, LANES), row),
                  pl.BlockSpec((tm, LANES), row),
                  pl.BlockSpec((1, Q_LORA_RANK), const),
                  pl.BlockSpec((1, KV_LORA_RANK), const),
                  pl.BlockSpec((None,) + wq.shape[1:], weight),
                  pl.BlockSpec((None,) + wkv.shape[1:], weight)],
        out_specs=[pl.BlockSpec((tm, MLA_WIDTH), row),
                   pl.BlockSpec((tm, rope_w), row),
                   pl.BlockSpec((tm, MLA_WIDTH), row),
                   pl.BlockSpec((tm, MLA_WIDTH), row),
                   pl.BlockSpec((tm, 2 * LANES), row)],
        out_shape=[jax.ShapeDtypeStruct((n, MLA_WIDTH), BF16),
                   jax.ShapeDtypeStruct((n, rope_w), BF16),
                   jax.ShapeDtypeStruct((n, MLA_WIDTH), BF16),
                   jax.ShapeDtypeStruct((n, MLA_WIDTH), BF16),
                   jax.ShapeDtypeStruct((n, 2 * LANES), BF16)],
        compiler_params=_params("parallel"),
        name="qkv_project",
    )(z, z, z, cos, sin, q_gain, kv_gain, wq, wkv)


def _scores(q, k):
    return lax.dot_general(q, k, (((1,), (1,)), ((), ())), preferred_element_type=F32)


def _online_softmax_update(s, v, state, mask):
    m_prev, l_prev, acc_prev = state
    if mask is not None:
        top = mask.shape[0]
        s_top = jnp.where(mask, s[:top], MASK_VALUE)
        s = s_top if s.shape[0] == top else jnp.concatenate([s_top, s[top:]], axis=0)
    m_next = jnp.maximum(m_prev, jnp.max(s, axis=1, keepdims=True))
    alpha = jnp.exp2(m_prev - m_next)
    ps = [jnp.exp2(s[:, c * LANES:(c + 1) * LANES] - m_next) for c in range(s.shape[1] // LANES)]
    l_next = alpha * l_prev + functools.reduce(lambda a, b: a + b, ps)
    p = jnp.concatenate(ps, axis=1).astype(BF16)
    acc_next = alpha * acc_prev + jnp.dot(p, v, preferred_element_type=F32)
    return m_next, l_next, acc_next


def _attn_kernel(qn_ref, qr_ref, kn_ref, kr_ref, v_ref, *rest, tq, tk, n_cast):
    w_refs = rest[:n_cast]
    o_ref = rest[n_cast]
    wo_refs = rest[n_cast + 1:2 * n_cast + 1]
    m_scr, l_scr, acc_scr = rest[2 * n_cast + 1:]
    for w_ref, wo_ref in zip(w_refs, wo_refs):
        wo_ref[...] = w_ref[...].astype(BF16)

    i = pl.program_id(2)
    per_tile = tq // tk
    m_scr[...] = jnp.full(m_scr.shape, MASK_VALUE, F32)
    l_scr[...] = jnp.zeros(l_scr.shape, F32)
    acc_scr[...] = jnp.zeros(acc_scr.shape, F32)

    def load_q(rows):
        return jnp.concatenate([qn_ref[rows, :], qr_ref[rows, :]], axis=1)

    def load_kv(start):
        k = jnp.concatenate([kn_ref[pl.ds(start, tk), :], kr_ref[pl.ds(start, tk), :]], axis=1)
        return k, v_ref[pl.ds(start, tk), :]

    def load_state(rows):
        return m_scr[rows, :], l_scr[rows, :], acc_scr[rows, :]

    def store_state(rows, state):
        m_scr[rows, :], l_scr[rows, :], acc_scr[rows, :] = state

    everything = slice(0, tq)

    def body(g, carry):
        q = load_q(everything)
        state = load_state(everything)
        for u in range(per_tile):
            k, v = load_kv(pl.multiple_of((g * per_tile + u) * tk, tk))
            state = _online_softmax_update(_scores(q, k), v, state, None)
        store_state(everything, state)
        return carry

    lax.fori_loop(0, i, body, 0)

    for d in range(per_tile):
        rows = slice(d * tk, tq)
        k, v = load_kv(pl.multiple_of(i * tq + d * tk, tk))
        row = lax.broadcasted_iota(jnp.int32, (tk, tk), 0)
        col = lax.broadcasted_iota(jnp.int32, (tk, tk), 1)
        state = _online_softmax_update(_scores(load_q(rows), k), v, load_state(rows), col <= row)
        store_state(rows, state)

    l = jnp.sum(l_scr[...], axis=1, keepdims=True)
    o_ref[...] = (acc_scr[...] / l).astype(o_ref.dtype)


def _slab_rows(total_rows, steps):
    rows = BF16_SUBLANES
    while total_rows % rows or total_rows // rows > steps:
        rows += BF16_SUBLANES
    return rows


def causal_attention(qn, qr, kn, kr, v, cast_weights, layer, batch, seq, tq, tk):
    n = qn.shape[0]
    nq = seq // tq
    steps = batch * N_HEADS * nq

    w_in_specs, w_out_specs, w_out_shapes = [], [], []
    for w in cast_weights:
        _, nrows, ncols = w.shape
        rows = _slab_rows(nrows, steps)
        last = nrows // rows - 1
        slab = lambda b, h, i, last=last: (jnp.minimum((b * N_HEADS + h) * nq + i, last), 0)
        w_in_specs.append(pl.BlockSpec((None, rows, ncols), lambda b, h, i, slab=slab: (layer,) + slab(b, h, i)))
        w_out_specs.append(pl.BlockSpec((rows, ncols), slab))
        w_out_shapes.append(jax.ShapeDtypeStruct((nrows, ncols), BF16))

    return pl.pallas_call(
        functools.partial(_attn_kernel, tq=tq, tk=tk, n_cast=len(cast_weights)),
        grid=(batch, N_HEADS, nq),
        in_specs=[pl.BlockSpec((tq, QK_NOPE_DIM), lambda b, h, i: (b * nq + i, h)),
                  pl.BlockSpec((tq, ROPE_PAIR_WIDTH), lambda b, h, i: (b * nq + i, h // 2)),
                  pl.BlockSpec((seq, QK_NOPE_DIM), lambda b, h, i: (b, h)),
                  pl.BlockSpec((seq, ROPE_PAIR_WIDTH), lambda b, h, i: (b, h % 2)),
                  pl.BlockSpec((seq, V_HEAD_DIM), lambda b, h, i: (b, h))] + w_in_specs,
        out_specs=[pl.BlockSpec((tq, V_HEAD_DIM), lambda b, h, i: (b * nq + i, h))] + w_out_specs,
        out_shape=[jax.ShapeDtypeStruct((n, MLA_WIDTH), BF16)] + w_out_shapes,
        scratch_shapes=[pltpu.VMEM((tq, LANES), F32),
                        pltpu.VMEM((tq, LANES), F32),
                        pltpu.VMEM((tq, V_HEAD_DIM), F32)],
        compiler_params=_params("arbitrary", "arbitrary", "arbitrary"),
        name="causal_attention",
    )(qn, qr, kn, kr, v, *cast_weights)


def _pool_kernel(u_ref, halo_ref, w_ref, scale_ref, o_ref, *, ts):
    i = pl.program_id(1)
    t = i * ts + lax.broadcasted_iota(jnp.int32, (ts, 1), 0)
    for g, window in enumerate(POOL_WINDOWS):
        sl = slice(g * POOL_GROUP_DIM, (g + 1) * POOL_GROUP_DIM)
        x = u_ref[:, sl]
        halo = jnp.where(i > 0, halo_ref[:, sl], 0.0)
        s = jnp.concatenate([halo, x], axis=0)
        shift = 1
        while shift < window:
            s = s + pltpu.roll(s, shift, 0)
            shift *= 2
        cnt = jnp.minimum(t + 1, window).astype(F32)
        d = s[POOL_HALO:, :] / cnt - x
        y = jnp.dot(d.astype(BF16), w_ref[g], preferred_element_type=F32)
        o_ref[:, sl] = (y * scale_ref[:, sl]).astype(o_ref.dtype)


def multiscale_pool(z, pool_w, pool_scale, layer, batch, seq, ts):
    n = z.shape[0]
    ns = seq // ts
    halo_blocks = ts // POOL_HALO
    return pl.pallas_call(
        functools.partial(_pool_kernel, ts=ts),
        grid=(batch, ns),
        in_specs=[pl.BlockSpec((ts, POOL_WIDTH), lambda b, i: (b * ns + i, Z_U // POOL_WIDTH)),
                  pl.BlockSpec((POOL_HALO, POOL_WIDTH),
                               lambda b, i: (jnp.maximum((b * ns + i) * halo_blocks - 1, 0), 0)),
                  pl.BlockSpec((None,) + pool_w.shape[1:], lambda b, i: (layer, 0, 0, 0)),
                  pl.BlockSpec((1, POOL_WIDTH), lambda b, i: (0, 0))],
        out_specs=pl.BlockSpec((ts, POOL_WIDTH), lambda b, i: (b * ns + i, 0)),
        out_shape=jax.ShapeDtypeStruct((n, POOL_WIDTH), BF16),
        compiler_params=_params("parallel", "parallel"),
        name="multiscale_pool",
    )(z, z, pool_w, pool_scale)


def _ffn_resid_kernel(a_ref, wg_ref, wu_ref, wd_ref, h_ref, gpost_ref, *rest,
                      chunk, nblocks, slabs, slab_rows, with_next):
    if with_next:
        gnext_ref, ho_ref, ao_ref, acc0, acc1 = rest
    else:
        ho_ref, acc0, acc1 = rest
        gnext_ref = ao_ref = None
    i = pl.program_id(0)
    f = pl.program_id(1)

    @pl.when(jnp.logical_and(i == 0, f == 0))
    def _():
        acc1[...] = jnp.zeros(acc1.shape, F32)

    def prepare(acc):
        @pl.when(f == 0)
        def _():
            acc[...] = jnp.zeros(acc.shape, F32)

    def matmuls(acc):
        a = a_ref[...]
        g = jnp.dot(a, wg_ref[...], preferred_element_type=F32)
        u = jnp.dot(a, wu_ref[...], preferred_element_type=F32)
        p = (g * (1.0 / (1.0 + jnp.exp(-g))) * u).astype(BF16)
        for c in range(acc.shape[1] // chunk):
            sl = slice(c * chunk, (c + 1) * chunk)
            acc[:, sl] += jnp.dot(p, wd_ref[:, sl], preferred_element_type=F32)

    def epilogue(acc):
        _resid_epilogue(lambda rows: acc[rows, :], f, slabs, slab_rows,
                        h_ref, gpost_ref, gnext_ref, ho_ref, ao_ref)

    _two_stream_step(i, nblocks, (acc0, acc1), matmuls, epilogue, prepare)


def swiglu_resid(a, w_gate, w_up, w_down, h, g_post, g_next, tm, tf):
    n, d = a.shape
    d_ff = w_gate.shape[1]
    nblocks = n // tm
    nf = d_ff // tf
    slab_rows = 2 * BF16_SUBLANES
    slabs = tm // slab_rows
    assert slabs <= nf and tm % slab_rows == 0
    with_next = g_next is not None

    def f_index(i, f):
        return jnp.where(i < nblocks, f, nf - 1)

    gain = pl.BlockSpec((1, d), lambda i, f: (0, 0))
    slab_spec = pl.BlockSpec((slab_rows, d), _lagged_slab_index(slabs))
    in_specs = [pl.BlockSpec((tm, d), lambda i, f: (jnp.minimum(i, nblocks - 1), 0),
                             pipeline_mode=pl.Buffered(1)),
                pl.BlockSpec((d, tf), lambda i, f: (0, f_index(i, f))),
                pl.BlockSpec((d, tf), lambda i, f: (0, f_index(i, f))),
                pl.BlockSpec((tf, d), lambda i, f: (f_index(i, f), 0)),
                slab_spec, gain]
    args = [a, w_gate, w_up, w_down, h, g_post]
    out_specs = [slab_spec]
    out_shape = [jax.ShapeDtypeStruct((n, d), F32)]
    if with_next:
        in_specs.append(gain)
        args.append(g_next)
        out_specs.append(slab_spec)
        out_shape.append(jax.ShapeDtypeStruct((n, d), BF16))
    outs = pl.pallas_call(
        functools.partial(_ffn_resid_kernel, chunk=512, nblocks=nblocks, slabs=slabs,
                          slab_rows=slab_rows, with_next=with_next),
        grid=(nblocks + 1, nf),
        in_specs=in_specs,
        out_specs=out_specs,
        out_shape=out_shape,
        scratch_shapes=[pltpu.VMEM((tm, d), F32), pltpu.VMEM((tm, d), F32)],
        compiler_params=pltpu.CompilerParams(dimension_semantics=("arbitrary", "arbitrary"),
                                             vmem_limit_bytes=TWO_ACC_VMEM_LIMIT_BYTES),
        name="swiglu_resid",
    )(*args)
    return (outs[0], outs[1]) if with_next else (outs[0], None)


def _prep_w_in(w_in):
    o1 = Q_LORA_RANK
    o2 = o1 + KV_LORA_RANK
    o3 = o2 + QK_ROPE_DIM
    w = w_in.astype(BF16)
    kr = w[..., o2:o3]
    return jnp.concatenate([w[..., o3:], w[..., o1:o2], kr, kr, w[..., :o1]], axis=-1)


def _prep_w_q(w_q_up):
    depth = w_q_up.shape[0]
    w = w_q_up.astype(BF16).reshape(depth, Q_LORA_RANK, N_HEADS, QK_NOPE_DIM + QK_ROPE_DIM)
    nope = w[..., :QK_NOPE_DIM].reshape(depth, Q_LORA_RANK, N_HEADS * QK_NOPE_DIM)
    rope = w[..., QK_NOPE_DIM:].reshape(depth, Q_LORA_RANK, N_HEADS * QK_ROPE_DIM)
    return jnp.concatenate([nope, rope], axis=-1)


def _prep_w_kv(w_kv_up):
    depth = w_kv_up.shape[0]
    w = w_kv_up.astype(BF16).reshape(depth, KV_LORA_RANK, N_HEADS, QK_NOPE_DIM + V_HEAD_DIM)
    k_nope = w[..., :QK_NOPE_DIM].reshape(depth, KV_LORA_RANK, N_HEADS * QK_NOPE_DIM)
    v = w[..., QK_NOPE_DIM:].reshape(depth, KV_LORA_RANK, N_HEADS * V_HEAD_DIM)
    return jnp.concatenate([k_nope, v], axis=-1)


def kernel(x, positions, w_in, q_norm, w_q_up, kv_norm, w_kv_up, pool_w, pool_scale, w_out,
           norm_pre_mix, norm_post_mix, norm_pre_ffn, norm_post_ffn, w_gate, w_up, w_down):
    batch, seq, d_model = x.shape
    depth = w_in.shape[0]
    n = batch * seq

    tm_big = min(1024, n)
    tm_mid = min(512, n)
    rows_rope = min(2048, n)
    tq = min(2048, seq)
    tk = min(256, seq)
    ts = min(512, seq)
    assert all(n % t == 0 for t in (tm_big, tm_mid, rows_rope)), (batch, seq)
    assert seq % tq == 0 and tq % tk == 0 and seq % ts == 0 and ts % POOL_HALO == 0, seq

    inv_freq = ROPE_THETA ** (-jnp.arange(0, QK_ROPE_DIM, 2, dtype=F32) / QK_ROPE_DIM)
    invf = jnp.tile(inv_freq, LANES // inv_freq.shape[0])[None, :]
    half = QK_ROPE_DIM // 2
    sign = jnp.where((jnp.arange(LANES) % QK_ROPE_DIM) < half, -1.0, 1.0).astype(F32)[None, :]
    cos, sin = rope_tables(positions.reshape(n, 1), invf, sign, rows_rope)

    gain = lambda g, l: g[l][None, :]
    h = x.reshape(n, d_model)
    a = None
    w_in_b = _prep_w_in(w_in)
    w_q_b = _prep_w_q(w_q_up)
    w_kv_b = _prep_w_kv(w_kv_up)
    pool_w_b = pool_w.astype(BF16)
    for l in range(depth):
        if a is None:
            z = norm_matmul(h, gain(norm_pre_mix, l), w_in_b, l, tm_big, 512)
        else:
            z = matmul(a, w_in_b, l, F32, tm_big, 512)
        qn, qr, kn, v, kr = qkv_project(z, cos, sin, gain(q_norm, l), gain(kv_norm, l),
                                        w_q_b, w_kv_b, l, tm_mid)
        attn, w_out_b, w_gate_b, w_up_b, w_down_b = causal_attention(
            qn, qr, kn, kr, v, (w_out, w_gate, w_up, w_down), l, batch, seq, tq, tk)
        pool = multiscale_pool(z, pool_w_b, gain(pool_scale, l), l, batch, seq, ts)
        h, a = outproj_resid(attn, pool, w_out_b, h, gain(norm_post_mix, l), gain(norm_pre_ffn, l),
                             tm_big, 256)
        g_next = gain(norm_pre_mix, l + 1) if l + 1 < depth else None
        h, a = swiglu_resid(a, w_gate_b, w_up_b, w_down_b, h, gain(norm_post_ffn, l), g_next, tm_big, 256)
    return h.reshape(batch, seq, d_model)
```

```python
import functools
import math

import jax
import jax.numpy as jnp
from jax import lax
from jax.experimental import pallas as pl
from jax.experimental.pallas import tpu as pltpu

F32 = jnp.float32
BF16 = jnp.bfloat16

N_HEADS = 16
QK_NOPE_DIM = 128
QK_ROPE_DIM = 64
V_HEAD_DIM = 128
Q_LORA_RANK = 896
KV_LORA_RANK = 512
POOL_WINDOWS = (2, 4, 8, 16)
POOL_GROUP_DIM = 512
POOL_WIDTH = POOL_GROUP_DIM * len(POOL_WINDOWS)
MLA_WIDTH = N_HEADS * V_HEAD_DIM
ROPE_PAIR_WIDTH = 2 * QK_ROPE_DIM
ATTN_SCALE = (QK_NOPE_DIM + QK_ROPE_DIM) ** -0.5
Q_PRESCALE = ATTN_SCALE * math.log2(math.e)
ROPE_THETA = 10000.0
NORM_EPS = 1e-6
MASK_VALUE = -1e30
ATTN_CHUNKS_PER_ITERATION = 8

LANES = 128
BF16_SUBLANES = 16
POOL_HALO = 16
VMEM_LIMIT_BYTES = 56 * 1024 * 1024
TWO_ACC_VMEM_LIMIT_BYTES = 60 * 1024 * 1024

Z_U = 0
Z_KV = POOL_WIDTH
Z_KR = Z_KV + KV_LORA_RANK
Z_Q = Z_KR + 2 * QK_ROPE_DIM
Z_WIDTH = Z_Q + Q_LORA_RANK


def _params(*sem):
    return pltpu.CompilerParams(dimension_semantics=sem, vmem_limit_bytes=VMEM_LIMIT_BYTES)


def _rms(x, g):
    ms = jnp.mean(x * x, axis=-1, keepdims=True)
    return x * lax.rsqrt(ms + NORM_EPS) * g


def _rope_table_kernel(pos_ref, invf_ref, sign_ref, cos_ref, sin_ref):
    ang = pos_ref[...].astype(F32) * invf_ref[...]
    cos_ref[...] = jnp.cos(ang)
    sin_ref[...] = jnp.sin(ang) * sign_ref[...]


def rope_tables(pos_col, invf, sign, rows):
    n = pos_col.shape[0]
    return pl.pallas_call(
        _rope_table_kernel,
        grid=(n // rows,),
        in_specs=[pl.BlockSpec((rows, 1), lambda i: (i, 0)),
                  pl.BlockSpec((1, LANES), lambda i: (0, 0)),
                  pl.BlockSpec((1, LANES), lambda i: (0, 0))],
        out_specs=[pl.BlockSpec((rows, LANES), lambda i: (i, 0))] * 2,
        out_shape=[jax.ShapeDtypeStruct((n, LANES), F32)] * 2,
        compiler_params=_params("parallel"),
        name="rope_tables",
    )(pos_col, invf, sign)


def _mm_kernel(a_ref, w_ref, o_ref):
    o_ref[...] = jnp.dot(a_ref[...], w_ref[...], preferred_element_type=F32).astype(o_ref.dtype)


def matmul(a, w, layer, out_dtype, tm, tn):
    m, k = a.shape
    n = w.shape[2]
    return pl.pallas_call(
        _mm_kernel,
        grid=(m // tm, n // tn),
        in_specs=[pl.BlockSpec((tm, k), lambda i, j: (i, 0)),
                  pl.BlockSpec((None, k, tn), lambda i, j: (layer, 0, j))],
        out_specs=pl.BlockSpec((tm, tn), lambda i, j: (i, j)),
        out_shape=jax.ShapeDtypeStruct((m, n), out_dtype),
        compiler_params=_params("parallel", "arbitrary"),
        name="matmul",
    )(a, w)


def _norm_matmul_kernel(x_ref, g_ref, w_ref, o_ref, a0, a1, *, nblocks, slabs, slab_rows):
    i = pl.program_id(0)
    j = pl.program_id(1)

    def produce(a_scr):
        @pl.when(j < slabs)
        def _():
            a_scr[j] = _rms(x_ref[...], g_ref[...]).astype(BF16)

    @pl.when(i == 0)
    def _():
        produce(a0)

    for parity, (a_cur, a_next) in enumerate(((a0, a1), (a1, a0))):
        @pl.when(jnp.logical_and(i > 0, (i - 1) % 2 == parity))
        def _(a_cur=a_cur, a_next=a_next):
            a = a_cur[...].reshape(slabs * slab_rows, a_cur.shape[2])
            o_ref[...] = jnp.dot(a, w_ref[...], preferred_element_type=F32)
            produce(a_next)


def norm_matmul(x, g, w, layer, tm, tn):
    m, k = x.shape
    n = w.shape[2]
    nblocks = m // tm
    nsteps = n // tn
    slab_rows = tm // 4
    slabs = tm // slab_rows
    assert slabs <= nsteps and slab_rows % BF16_SUBLANES == 0

    def x_index(i, j):
        return (jnp.minimum(i, nblocks - 1) * slabs + jnp.minimum(j, slabs - 1), 0)

    col = lambda i, j: jnp.where(i == 0, 0, j)
    return pl.pallas_call(
        functools.partial(_norm_matmul_kernel, nblocks=nblocks, slabs=slabs, slab_rows=slab_rows),
        grid=(nblocks + 1, nsteps),
        in_specs=[pl.BlockSpec((slab_rows, k), x_index),
                  pl.BlockSpec((1, k), lambda i, j: (0, 0)),
                  pl.BlockSpec((None, k, tn), lambda i, j: (layer, 0, col(i, j)))],
        out_specs=pl.BlockSpec((tm, tn), lambda i, j: (jnp.maximum(i - 1, 0), col(i, j))),
        out_shape=jax.ShapeDtypeStruct((m, n), F32),
        scratch_shapes=[pltpu.VMEM((slabs, slab_rows, k), BF16), pltpu.VMEM((slabs, slab_rows, k), BF16)],
        compiler_params=_params("arbitrary", "arbitrary"),
        name="norm_matmul",
    )(x, g, w)


def _resid_epilogue(read_rows, step, slabs, slab_rows, h_ref, gpost_ref, gnext_ref, ho_ref, ao_ref):
    slab = jnp.minimum(step, slabs - 1)
    rows = pl.ds(pl.multiple_of(slab * slab_rows, slab_rows), slab_rows)
    h = h_ref[...] + _rms(read_rows(rows), gpost_ref[...])
    ho_ref[...] = h
    if ao_ref is not None:
        ao_ref[...] = _rms(h, gnext_ref[...]).astype(BF16)


def _two_stream_step(i, nblocks, accs, matmuls, epilogue, prepare=None):
    for parity in (0, 1):
        @pl.when(jnp.logical_and(i < nblocks, i % 2 == parity))
        def _(parity=parity):
            if prepare is not None:
                prepare(accs[parity])
            epilogue(accs[1 - parity])
            matmuls(accs[parity])

    @pl.when(i == nblocks)
    def _():
        epilogue(accs[(nblocks - 1) % 2])


def _lagged_slab_index(slabs):
    def index(i, j):
        return (jnp.maximum((i - 1) * slabs + jnp.minimum(j, slabs - 1), 0), 0)
    return index


def _outproj_resid_kernel(a1_ref, a2_ref, w1_ref, w2_ref, h_ref, gpost_ref, gnext_ref,
                          ho_ref, ao_ref, acc0, acc1, *, nblocks, slabs, slab_rows):
    i = pl.program_id(0)
    j = pl.program_id(1)
    nsteps = acc0.shape[0]

    @pl.when(jnp.logical_and(i == 0, j == 0))
    def _():
        acc1[...] = jnp.zeros(acc1.shape, F32)

    def matmuls(acc):
        acc[j] = (jnp.dot(a1_ref[...], w1_ref[...], preferred_element_type=F32)
                  + jnp.dot(a2_ref[...], w2_ref[...], preferred_element_type=F32))

    def epilogue(acc):
        read_rows = lambda rows: jnp.concatenate([acc[c, rows, :] for c in range(nsteps)], axis=1)
        _resid_epilogue(read_rows, j, slabs, slab_rows, h_ref, gpost_ref, gnext_ref, ho_ref, ao_ref)

    _two_stream_step(i, nblocks, (acc0, acc1), matmuls, epilogue)


def outproj_resid(a1, a2, w, h, g_post, g_next, tm, tn):
    m, k1 = a1.shape
    k2 = a2.shape[1]
    d = w.shape[1]
    assert k1 == k2 and w.shape[0] == k1 + k2
    nblocks = m // tm
    nsteps = d // tn
    slabs = nsteps
    slab_rows = tm // slabs
    assert tm % slabs == 0 and slab_rows % BF16_SUBLANES == 0

    block_row = lambda i, j: (jnp.minimum(i, nblocks - 1), 0)
    w_col = lambda i, j: jnp.where(i < nblocks, j, nsteps - 1)
    gain = pl.BlockSpec((1, d), lambda i, j: (0, 0))
    slab_spec = pl.BlockSpec((slab_rows, d), _lagged_slab_index(slabs))
    return pl.pallas_call(
        functools.partial(_outproj_resid_kernel, nblocks=nblocks, slabs=slabs, slab_rows=slab_rows),
        grid=(nblocks + 1, nsteps),
        in_specs=[pl.BlockSpec((tm, k1), block_row),
                  pl.BlockSpec((tm, k2), block_row),
                  pl.BlockSpec((k1, tn), lambda i, j: (0, w_col(i, j))),
                  pl.BlockSpec((k2, tn), lambda i, j: (1, w_col(i, j))),
                  slab_spec, gain, gain],
        out_specs=[slab_spec, slab_spec],
        out_shape=[jax.ShapeDtypeStruct((m, d), F32), jax.ShapeDtypeStruct((m, d), BF16)],
        scratch_shapes=[pltpu.VMEM((nsteps, tm, tn), F32), pltpu.VMEM((nsteps, tm, tn), F32)],
        compiler_params=pltpu.CompilerParams(dimension_semantics=("arbitrary", "arbitrary"),
                                             vmem_limit_bytes=TWO_ACC_VMEM_LIMIT_BYTES),
        name="outproj_resid",
    )(a1, a2, w, w, h, g_post, g_next)


def _qkv_kernel(cq_ref, ckv_ref, kr_ref, cos_ref, sin_ref, qg_ref, kvg_ref, wq_ref, wkv_ref,
                qn_ref, qr_ref, kn_ref, v_ref, kro_ref):
    cos = cos_ref[...]
    sin = sin_ref[...]
    tm = cos.shape[0]
    lane = lax.broadcasted_iota(jnp.int32, (tm, LANES), 1)
    first_half = (lane % QK_ROPE_DIM) < (QK_ROPE_DIM // 2)
    chunk = 512

    cqn = _rms(cq_ref[...], qg_ref[...]).astype(BF16)
    for c in range(MLA_WIDTH // chunk):
        sl = slice(c * chunk, (c + 1) * chunk)
        qn = jnp.dot(cqn, wq_ref[:, sl], preferred_element_type=F32)
        qn_ref[:, sl] = (qn * Q_PRESCALE).astype(BF16)
    t_all = jnp.dot(cqn, wq_ref[:, MLA_WIDTH:], preferred_element_type=F32)
    for c in range(t_all.shape[1] // LANES):
        t = t_all[:, c * LANES:(c + 1) * LANES]
        swapped = jnp.where(first_half, pltpu.roll(t, LANES - 32, 1), pltpu.roll(t, 32, 1))
        qr_ref[:, c * LANES:(c + 1) * LANES] = ((t * cos + swapped * sin) * Q_PRESCALE).astype(BF16)

    ckvn = _rms(ckv_ref[...], kvg_ref[...]).astype(BF16)
    per_head = QK_NOPE_DIM + V_HEAD_DIM
    heads_per_chunk = chunk // per_head
    for c in range(N_HEADS // heads_per_chunk):
        kv = jnp.dot(ckvn, wkv_ref[:, c * chunk:(c + 1) * chunk], preferred_element_type=F32).astype(BF16)
        for hh in range(heads_per_chunk):
            head = c * heads_per_chunk + hh
            out = slice(head * QK_NOPE_DIM, (head + 1) * QK_NOPE_DIM)
            kn_ref[:, out] = kv[:, hh * per_head:hh * per_head + QK_NOPE_DIM]
            v_ref[:, out] = kv[:, hh * per_head + QK_NOPE_DIM:(hh + 1) * per_head]

    kr = kr_ref[...]
    kr_rot = kr * cos + pltpu.roll(kr, 32, 1) * sin
    zero = jnp.zeros_like(kr_rot)
    kro_ref[:, :LANES] = jnp.where(lane < QK_ROPE_DIM, kr_rot, zero).astype(BF16)
    kro_ref[:, LANES:] = jnp.where(lane >= QK_ROPE_DIM, kr_rot, zero).astype(BF16)


def qkv_project(z, cos, sin, q_gain, kv_gain, wq, wkv, layer, tm):
    n = z.shape[0]
    rope_w = N_HEADS * QK_ROPE_DIM
    row = lambda i: (i, 0)
    const = lambda i: (0, 0)
    weight = lambda i: (layer, 0, 0)
    return pl.pallas_call(
        _qkv_kernel,
        grid=(n // tm,),
        in_specs=[pl.BlockSpec((tm, Q_LORA_RANK), lambda i: (i, Z_Q // Q_LORA_RANK)),
                  pl.BlockSpec((tm, KV_LORA_RANK), lambda i: (i, Z_KV // KV_LORA_RANK)),
                  pl.BlockSpec((tm, LANES), lambda i: (i, Z_KR // LANES)),
                  pl.BlockSpec((tm, LANES), row),
                  pl.BlockSpec((tm, LANES), row),
                  pl.BlockSpec((1, Q_LORA_RANK), const),
                  pl.BlockSpec((1, KV_LORA_RANK), const),
                  pl.BlockSpec((None,) + wq.shape[1:], weight),
                  pl.BlockSpec((None,) + wkv.shape[1:], weight)],
        out_specs=[pl.BlockSpec((tm, MLA_WIDTH), row),
                   pl.BlockSpec((tm, rope_w), row),
                   pl.BlockSpec((tm, MLA_WIDTH), row),
                   pl.BlockSpec((tm, MLA_WIDTH), row),
                   pl.BlockSpec((tm, 2 * LANES), row)],
        out_shape=[jax.ShapeDtypeStruct((n, MLA_WIDTH), BF16),
                   jax.ShapeDtypeStruct((n, rope_w), BF16),
                   jax.ShapeDtypeStruct((n, MLA_WIDTH), BF16),
                   jax.ShapeDtypeStruct((n, MLA_WIDTH), BF16),
                   jax.ShapeDtypeStruct((n, 2 * LANES), BF16)],
        compiler_params=_params("parallel"),
        name="qkv_project",
    )(z, z, z, cos, sin, q_gain, kv_gain, wq, wkv)


def _scores(q, k):
    return lax.dot_general(q, k, (((1,), (1,)), ((), ())), preferred_element_type=F32)


def _online_softmax_update(s, v, state, mask):
    m_prev, l_prev, acc_prev = state
    if mask is not None:
        top = mask.shape[0]
        s_top = jnp.where(mask, s[:top], MASK_VALUE)
        s = s_top if s.shape[0] == top else jnp.concatenate([s_top, s[top:]], axis=0)
    m_next = jnp.maximum(m_prev, jnp.max(s, axis=1, keepdims=True))
    alpha = jnp.exp2(m_prev - m_next)
    ps = [jnp.exp2(s[:, c * LANES:(c + 1) * LANES] - m_next) for c in range(s.shape[1] // LANES)]
    l_next = alpha * l_prev + functools.reduce(lambda a, b: a + b, ps)
    p = jnp.concatenate(ps, axis=1).astype(BF16)
    acc_next = alpha * acc_prev + jnp.dot(p, v, preferred_element_type=F32)
    return m_next, l_next, acc_next


def _attn_kernel(qn_ref, qr_ref, kn_ref, kr_ref, v_ref, *rest, tq, tk, n_cast):
    w_refs = rest[:n_cast]
    o_ref = rest[n_cast]
    wo_refs = rest[n_cast + 1:2 * n_cast + 1]
    m_scr, l_scr, acc_scr = rest[2 * n_cast + 1:]
    for w_ref, wo_ref in zip(w_refs, wo_refs):
        wo_ref[...] = w_ref[...].astype(BF16)

    i = pl.program_id(2)
    per_tile = tq // tk
    group = min(per_tile, ATTN_CHUNKS_PER_ITERATION)
    m_scr[...] = jnp.full(m_scr.shape, MASK_VALUE, F32)
    l_scr[...] = jnp.zeros(l_scr.shape, F32)
    acc_scr[...] = jnp.zeros(acc_scr.shape, F32)

    def load_q(rows):
        return jnp.concatenate([qn_ref[rows, :], qr_ref[rows, :]], axis=1)

    def load_kv(start):
        k = jnp.concatenate([kn_ref[pl.ds(start, tk), :], kr_ref[pl.ds(start, tk), :]], axis=1)
        return k, v_ref[pl.ds(start, tk), :]

    def load_state(rows):
        return m_scr[rows, :], l_scr[rows, :], acc_scr[rows, :]

    def store_state(rows, state):
        m_scr[rows, :], l_scr[rows, :], acc_scr[rows, :] = state

    everything = slice(0, tq)

    def body(g, carry):
        q = load_q(everything)
        state = load_state(everything)
        for u in range(group):
            k, v = load_kv(pl.multiple_of((g * group + u) * tk, tk))
            state = _online_softmax_update(_scores(q, k), v, state, None)
        store_state(everything, state)
        return carry

    lax.fori_loop(0, i * (per_tile // group), body, 0)

    for d in range(per_tile):
        rows = slice(d * tk, tq)
        k, v = load_kv(pl.multiple_of(i * tq + d * tk, tk))
        row = lax.broadcasted_iota(jnp.int32, (tk, tk), 0)
        col = lax.broadcasted_iota(jnp.int32, (tk, tk), 1)
        state = _online_softmax_update(_scores(load_q(rows), k), v, load_state(rows), col <= row)
        store_state(rows, state)

    l = jnp.sum(l_scr[...], axis=1, keepdims=True)
    o_ref[...] = (acc_scr[...] / l).astype(o_ref.dtype)


def _slab_rows(total_rows, steps):
    rows = BF16_SUBLANES
    while total_rows % rows or total_rows // rows > steps:
        rows += BF16_SUBLANES
    return rows


def causal_attention(qn, qr, kn, kr, v, cast_weights, layer, batch, seq, tq, tk):
    n = qn.shape[0]
    nq = seq // tq
    steps = batch * N_HEADS * nq

    w_in_specs, w_out_specs, w_out_shapes = [], [], []
    for w in cast_weights:
        _, nrows, ncols = w.shape
        rows = _slab_rows(nrows, steps)
        last = nrows // rows - 1
        slab = lambda b, h, i, last=last: (jnp.minimum((b * N_HEADS + h) * nq + i, last), 0)
        w_in_specs.append(pl.BlockSpec((None, rows, ncols), lambda b, h, i, slab=slab: (layer,) + slab(b, h, i)))
        w_out_specs.append(pl.BlockSpec((rows, ncols), slab))
        w_out_shapes.append(jax.ShapeDtypeStruct((nrows, ncols), BF16))

    return pl.pallas_call(
        functools.partial(_attn_kernel, tq=tq, tk=tk, n_cast=len(cast_weights)),
        grid=(batch, N_HEADS, nq),
        in_specs=[pl.BlockSpec((tq, QK_NOPE_DIM), lambda b, h, i: (b * nq + i, h)),
                  pl.BlockSpec((tq, ROPE_PAIR_WIDTH), lambda b, h, i: (b * nq + i, h // 2)),
                  pl.BlockSpec((seq, QK_NOPE_DIM), lambda b, h, i: (b, h)),
                  pl.BlockSpec((seq, ROPE_PAIR_WIDTH), lambda b, h, i: (b, h % 2)),
                  pl.BlockSpec((seq, V_HEAD_DIM), lambda b, h, i: (b, h))] + w_in_specs,
        out_specs=[pl.BlockSpec((tq, V_HEAD_DIM), lambda b, h, i: (b * nq + i, h))] + w_out_specs,
        out_shape=[jax.ShapeDtypeStruct((n, MLA_WIDTH), BF16)] + w_out_shapes,
        scratch_shapes=[pltpu.VMEM((tq, LANES), F32),
                        pltpu.VMEM((tq, LANES), F32),
                        pltpu.VMEM((tq, V_HEAD_DIM), F32)],
        compiler_params=_params("arbitrary", "arbitrary", "arbitrary"),
        name="causal_attention",
    )(qn, qr, kn, kr, v, *cast_weights)


def _pool_kernel(u_ref, halo_ref, w_ref, scale_ref, o_ref, *, ts):
    i = pl.program_id(1)
    t = i * ts + lax.broadcasted_iota(jnp.int32, (ts, 1), 0)
    for g, window in enumerate(POOL_WINDOWS):
        sl = slice(g * POOL_GROUP_DIM, (g + 1) * POOL_GROUP_DIM)
        x = u_ref[:, sl]
        halo = jnp.where(i > 0, halo_ref[:, sl], 0.0)
        s = jnp.concatenate([halo, x], axis=0)
        shift = 1
        while shift < window:
            s = s + pltpu.roll(s, shift, 0)
            shift *= 2
        cnt = jnp.minimum(t + 1, window).astype(F32)
        d = s[POOL_HALO:, :] / cnt - x
        y = jnp.dot(d.astype(BF16), w_ref[g], preferred_element_type=F32)
        o_ref[:, sl] = (y * scale_ref[:, sl]).astype(o_ref.dtype)


def multiscale_pool(z, pool_w, pool_scale, layer, batch, seq, ts):
    n = z.shape[0]
    ns = seq // ts
    halo_blocks = ts // POOL_HALO
    return pl.pallas_call(
        functools.partial(_pool_kernel, ts=ts),
        grid=(batch, ns),
        in_specs=[pl.BlockSpec((ts, POOL_WIDTH), lambda b, i: (b * ns + i, Z_U // POOL_WIDTH)),
                  pl.BlockSpec((POOL_HALO, POOL_WIDTH),
                               lambda b, i: (jnp.maximum((b * ns + i) * halo_blocks - 1, 0), 0)),
                  pl.BlockSpec((None,) + pool_w.shape[1:], lambda b, i: (layer, 0, 0, 0)),
                  pl.BlockSpec((1, POOL_WIDTH), lambda b, i: (0, 0))],
        out_specs=pl.BlockSpec((ts, POOL_WIDTH), lambda b, i: (b * ns + i, 0)),
        out_shape=jax.ShapeDtypeStruct((n, POOL_WIDTH), BF16),
        compiler_params=_params("parallel", "parallel"),
        name="multiscale_pool",
    )(z, z, pool_w, pool_scale)


def _ffn_resid_kernel(a_ref, wg_ref, wu_ref, wd_ref, h_ref, gpost_ref, *rest,
                      chunk, nblocks, slabs, slab_rows, with_next):
    if with_next:
        gnext_ref, ho_ref, ao_ref, acc0, acc1 = rest
    else:
        ho_ref, acc0, acc1 = rest
        gnext_ref = ao_ref = None
    i = pl.program_id(0)
    f = pl.program_id(1)

    @pl.when(jnp.logical_and(i == 0, f == 0))
    def _():
        acc1[...] = jnp.zeros(acc1.shape, F32)

    def prepare(acc):
        @pl.when(f == 0)
        def _():
            acc[...] = jnp.zeros(acc.shape, F32)

    def matmuls(acc):
        a = a_ref[...]
        g = jnp.dot(a, wg_ref[...], preferred_element_type=F32)
        u = jnp.dot(a, wu_ref[...], preferred_element_type=F32)
        p = (g * (1.0 / (1.0 + jnp.exp(-g))) * u).astype(BF16)
        for c in range(acc.shape[1] // chunk):
            sl = slice(c * chunk, (c + 1) * chunk)
            acc[:, sl] += jnp.dot(p, wd_ref[:, sl], preferred_element_type=F32)

    def epilogue(acc):
        _resid_epilogue(lambda rows: acc[rows, :], f, slabs, slab_rows,
                        h_ref, gpost_ref, gnext_ref, ho_ref, ao_ref)

    _two_stream_step(i, nblocks, (acc0, acc1), matmuls, epilogue, prepare)


def swiglu_resid(a, w_gate, w_up, w_down, h, g_post, g_next, tm, tf):
    n, d = a.shape
    d_ff = w_gate.shape[1]
    nblocks = n // tm
    nf = d_ff // tf
    slab_rows = 2 * BF16_SUBLANES
    slabs = tm // slab_rows
    assert slabs <= nf and tm % slab_rows == 0
    with_next = g_next is not None

    def f_index(i, f):
        return jnp.where(i < nblocks, f, nf - 1)

    gain = pl.BlockSpec((1, d), lambda i, f: (0, 0))
    slab_spec = pl.BlockSpec((slab_rows, d), _lagged_slab_index(slabs))
    in_specs = [pl.BlockSpec((tm, d), lambda i, f: (jnp.minimum(i, nblocks - 1), 0),
                             pipeline_mode=pl.Buffered(1)),
                pl.BlockSpec((d, tf), lambda i, f: (0, f_index(i, f))),
                pl.BlockSpec((d, tf), lambda i, f: (0, f_index(i, f))),
                pl.BlockSpec((tf, d), lambda i, f: (f_index(i, f), 0)),
                slab_spec, gain]
    args = [a, w_gate, w_up, w_down, h, g_post]
    out_specs = [slab_spec]
    out_shape = [jax.ShapeDtypeStruct((n, d), F32)]
    if with_next:
        in_specs.append(gain)
        args.append(g_next)
        out_specs.append(slab_spec)
        out_shape.append(jax.ShapeDtypeStruct((n, d), BF16))
    outs = pl.pallas_call(
        functools.partial(_ffn_resid_kernel, chunk=512, nblocks=nblocks, slabs=slabs,
                          slab_rows=slab_rows, with_next=with_next),
        grid=(nblocks + 1, nf),
        in_specs=in_specs,
        out_specs=out_specs,
        out_shape=out_shape,
        scratch_shapes=[pltpu.VMEM((tm, d), F32), pltpu.VMEM((tm, d), F32)],
        compiler_params=pltpu.CompilerParams(dimension_semantics=("arbitrary", "arbitrary"),
                                             vmem_limit_bytes=TWO_ACC_VMEM_LIMIT_BYTES),
        name="swiglu_resid",
    )(*args)
    return (outs[0], outs[1]) if with_next else (outs[0], None)


def _prep_w_in(w_in):
    o1 = Q_LORA_RANK
    o2 = o1 + KV_LORA_RANK
    o3 = o2 + QK_ROPE_DIM
    w = w_in.astype(BF16)
    kr = w[..., o2:o3]
    return jnp.concatenate([w[..., o3:], w[..., o1:o2], kr, kr, w[..., :o1]], axis=-1)


def _prep_w_q(w_q_up):
    depth = w_q_up.shape[0]
    w = w_q_up.astype(BF16).reshape(depth, Q_LORA_RANK, N_HEADS, QK_NOPE_DIM + QK_ROPE_DIM)
    nope = w[..., :QK_NOPE_DIM].reshape(depth, Q_LORA_RANK, N_HEADS * QK_NOPE_DIM)
    rope = w[..., QK_NOPE_DIM:].reshape(depth, Q_LORA_RANK, N_HEADS * QK_ROPE_DIM)
    return jnp.concatenate([nope, rope], axis=-1)


def kernel(x, positions, w_in, q_norm, w_q_up, kv_norm, w_kv_up, pool_w, pool_scale, w_out,
           norm_pre_mix, norm_post_mix, norm_pre_ffn, norm_post_ffn, w_gate, w_up, w_down):
    batch, seq, d_model = x.shape
    depth = w_in.shape[0]
    n = batch * seq

    tm_big = min(1024, n)
    tm_mid = min(512, n)
    rows_rope = min(2048, n)
    tq = min(2048, seq)
    tk = min(256, seq)
    ts = min(512, seq)
    assert all(n % t == 0 for t in (tm_big, tm_mid, rows_rope)), (batch, seq)
    assert seq % tq == 0 and tq % tk == 0 and seq % ts == 0 and ts % POOL_HALO == 0, seq

    inv_freq = ROPE_THETA ** (-jnp.arange(0, QK_ROPE_DIM, 2, dtype=F32) / QK_ROPE_DIM)
    invf = jnp.tile(inv_freq, LANES // inv_freq.shape[0])[None, :]
    half = QK_ROPE_DIM // 2
    sign = jnp.where((jnp.arange(LANES) % QK_ROPE_DIM) < half, -1.0, 1.0).astype(F32)[None, :]
    cos, sin = rope_tables(positions.reshape(n, 1), invf, sign, rows_rope)

    gain = lambda g, l: g[l][None, :]
    h = x.reshape(n, d_model)
    a = None
    w_in_b = _prep_w_in(w_in)
    w_q_b = _prep_w_q(w_q_up)
    w_kv_b = w_kv_up.astype(BF16)
    pool_w_b = pool_w.astype(BF16)
    for l in range(depth):
        if a is None:
            z = norm_matmul(h, gain(norm_pre_mix, l), w_in_b, l, tm_big, 512)
        else:
            z = matmul(a, w_in_b, l, F32, tm_big, 512)
        qn, qr, kn, v, kr = qkv_project(z, cos, sin, gain(q_norm, l), gain(kv_norm, l),
                                        w_q_b, w_kv_b, l, tm_mid)
        attn, w_out_b, w_gate_b, w_up_b, w_down_b = causal_attention(
            qn, qr, kn, kr, v, (w_out, w_gate, w_up, w_down), l, batch, seq, tq, tk)
        pool = multiscale_pool(z, pool_w_b, gain(pool_scale, l), l, batch, seq, ts)
        h, a = outproj_resid(attn, pool, w_out_b, h, gain(norm_post_mix, l), gain(norm_pre_ffn, l),
                             tm_big, 256)
        g_next = gain(norm_pre_mix, l + 1) if l + 1 < depth else None
        h, a = swiglu_resid(a, w_gate_b, w_up_b, w_down_b, h, gain(norm_post_ffn, l), g_next, tm_big, 256)
    return h.reshape(batch, seq, d_model)
```

```python
import functools
import math

import jax
import jax.numpy as jnp
from jax import lax
from jax.experimental import pallas as pl
from jax.experimental.pallas import tpu as pltpu

F32 = jnp.float32
BF16 = jnp.bfloat16

N_HEADS = 16
QK_NOPE_DIM = 128
QK_ROPE_DIM = 64
V_HEAD_DIM = 128
Q_LORA_RANK = 896
KV_LORA_RANK = 512
POOL_WINDOWS = (2, 4, 8, 16)
POOL_GROUP_DIM = 512
POOL_WIDTH = POOL_GROUP_DIM * len(POOL_WINDOWS)
MLA_WIDTH = N_HEADS * V_HEAD_DIM
ROPE_PAIR_WIDTH = 2 * QK_ROPE_DIM
ATTN_SCALE = (QK_NOPE_DIM + QK_ROPE_DIM) ** -0.5
Q_PRESCALE = ATTN_SCALE * math.log2(math.e)
ROPE_THETA = 10000.0
NORM_EPS = 1e-6
MASK_VALUE = -1e30
ATTN_CHUNKS_PER_ITERATION = 8

LANES = 128
BF16_SUBLANES = 16
POOL_HALO = 16
VMEM_LIMIT_BYTES = 56 * 1024 * 1024
TWO_ACC_VMEM_LIMIT_BYTES = 60 * 1024 * 1024

Z_U = 0
Z_KV = POOL_WIDTH
Z_KR = Z_KV + KV_LORA_RANK
Z_Q = Z_KR + 2 * QK_ROPE_DIM
Z_WIDTH = Z_Q + Q_LORA_RANK


def _params(*sem):
    return pltpu.CompilerParams(dimension_semantics=sem, vmem_limit_bytes=VMEM_LIMIT_BYTES)


def _rms(x, g):
    ms = jnp.mean(x * x, axis=-1, keepdims=True)
    return x * lax.rsqrt(ms + NORM_EPS) * g


def _rope_table_kernel(pos_ref, invf_ref, sign_ref, cos_ref, sin_ref):
    ang = pos_ref[...].astype(F32) * invf_ref[...]
    cos_ref[...] = jnp.cos(ang)
    sin_ref[...] = jnp.sin(ang) * sign_ref[...]


def rope_tables(pos_col, invf, sign, rows):
    n = pos_col.shape[0]
    return pl.pallas_call(
        _rope_table_kernel,
        grid=(n // rows,),
        in_specs=[pl.BlockSpec((rows, 1), lambda i: (i, 0)),
                  pl.BlockSpec((1, LANES), lambda i: (0, 0)),
                  pl.BlockSpec((1, LANES), lambda i: (0, 0))],
        out_specs=[pl.BlockSpec((rows, LANES), lambda i: (i, 0))] * 2,
        out_shape=[jax.ShapeDtypeStruct((n, LANES), F32)] * 2,
        compiler_params=_params("parallel"),
        name="rope_tables",
    )(pos_col, invf, sign)


def _mm_kernel(a_ref, w_ref, o_ref):
    o_ref[...] = jnp.dot(a_ref[...], w_ref[...], preferred_element_type=F32).astype(o_ref.dtype)


def matmul(a, w, layer, out_dtype, tm, tn):
    m, k = a.shape
    n = w.shape[2]
    return pl.pallas_call(
        _mm_kernel,
        grid=(m // tm, n // tn),
        in_specs=[pl.BlockSpec((tm, k), lambda i, j: (i, 0)),
                  pl.BlockSpec((None, k, tn), lambda i, j: (layer, 0, j))],
        out_specs=pl.BlockSpec((tm, tn), lambda i, j: (i, j)),
        out_shape=jax.ShapeDtypeStruct((m, n), out_dtype),
        compiler_params=_params("parallel", "arbitrary"),
        name="matmul",
    )(a, w)


def _norm_matmul_kernel(x_ref, g_ref, w_ref, o_ref, a0, a1, *, nblocks, slabs, slab_rows):
    i = pl.program_id(0)
    j = pl.program_id(1)

    def produce(a_scr):
        @pl.when(j < slabs)
        def _():
            a_scr[j] = _rms(x_ref[...], g_ref[...]).astype(BF16)

    @pl.when(i == 0)
    def _():
        produce(a0)

    for parity, (a_cur, a_next) in enumerate(((a0, a1), (a1, a0))):
        @pl.when(jnp.logical_and(i > 0, (i - 1) % 2 == parity))
        def _(a_cur=a_cur, a_next=a_next):
            a = a_cur[...].reshape(slabs * slab_rows, a_cur.shape[2])
            o_ref[...] = jnp.dot(a, w_ref[...], preferred_element_type=F32)
            produce(a_next)


def norm_matmul(x, g, w, layer, tm, tn):
    m, k = x.shape
    n = w.shape[2]
    nblocks = m // tm
    nsteps = n // tn
    slab_rows = tm // 4
    slabs = tm // slab_rows
    assert slabs <= nsteps and slab_rows % BF16_SUBLANES == 0

    def x_index(i, j):
        return (jnp.minimum(i, nblocks - 1) * slabs + jnp.minimum(j, slabs - 1), 0)

    col = lambda i, j: jnp.where(i == 0, 0, j)
    return pl.pallas_call(
        functools.partial(_norm_matmul_kernel, nblocks=nblocks, slabs=slabs, slab_rows=slab_rows),
        grid=(nblocks + 1, nsteps),
        in_specs=[pl.BlockSpec((slab_rows, k), x_index),
                  pl.BlockSpec((1, k), lambda i, j: (0, 0)),
                  pl.BlockSpec((None, k, tn), lambda i, j: (layer, 0, col(i, j)))],
        out_specs=pl.BlockSpec((tm, tn), lambda i, j: (jnp.maximum(i - 1, 0), col(i, j))),
        out_shape=jax.ShapeDtypeStruct((m, n), F32),
        scratch_shapes=[pltpu.VMEM((slabs, slab_rows, k), BF16), pltpu.VMEM((slabs, slab_rows, k), BF16)],
        compiler_params=_params("arbitrary", "arbitrary"),
        name="norm_matmul",
    )(x, g, w)


def _resid_epilogue(read_rows, step, slabs, slab_rows, h_ref, gpost_ref, gnext_ref, ho_ref, ao_ref):
    slab = jnp.minimum(step, slabs - 1)
    rows = pl.ds(pl.multiple_of(slab * slab_rows, slab_rows), slab_rows)
    h = h_ref[...] + _rms(read_rows(rows), gpost_ref[...])
    ho_ref[...] = h
    if ao_ref is not None:
        ao_ref[...] = _rms(h, gnext_ref[...]).astype(BF16)


def _two_stream_step(i, nblocks, accs, matmuls, epilogue, prepare=None):
    for parity in (0, 1):
        @pl.when(jnp.logical_and(i < nblocks, i % 2 == parity))
        def _(parity=parity):
            if prepare is not None:
                prepare(accs[parity])
            epilogue(accs[1 - parity])
            matmuls(accs[parity])

    @pl.when(i == nblocks)
    def _():
        epilogue(accs[(nblocks - 1) % 2])


def _lagged_slab_index(slabs):
    def index(i, j):
        return (jnp.maximum((i - 1) * slabs + jnp.minimum(j, slabs - 1), 0), 0)
    return index


def _outproj_resid_kernel(a1_ref, a2_ref, w1_ref, w2_ref, h_ref, gpost_ref, gnext_ref,
                          ho_ref, ao_ref, acc0, acc1, *, nblocks, slabs, slab_rows):
    i = pl.program_id(0)
    j = pl.program_id(1)
    nsteps = acc0.shape[0]

    @pl.when(jnp.logical_and(i == 0, j == 0))
    def _():
        acc1[...] = jnp.zeros(acc1.shape, F32)

    def matmuls(acc):
        acc[j] = (jnp.dot(a1_ref[...], w1_ref[...], preferred_element_type=F32)
                  + jnp.dot(a2_ref[...], w2_ref[...], preferred_element_type=F32))

    def epilogue(acc):
        read_rows = lambda rows: jnp.concatenate([acc[c, rows, :] for c in range(nsteps)], axis=1)
        _resid_epilogue(read_rows, j, slabs, slab_rows, h_ref, gpost_ref, gnext_ref, ho_ref, ao_ref)

    _two_stream_step(i, nblocks, (acc0, acc1), matmuls, epilogue)


def outproj_resid(a1, a2, w, h, g_post, g_next, tm, tn):
    m, k1 = a1.shape
    k2 = a2.shape[1]
    d = w.shape[1]
    assert k1 == k2 and w.shape[0] == k1 + k2
    nblocks = m // tm
    nsteps = d // tn
    slabs = nsteps
    slab_rows = tm // slabs
    assert tm % slabs == 0 and slab_rows % BF16_SUBLANES == 0

    block_row = lambda i, j: (jnp.minimum(i, nblocks - 1), 0)
    w_col = lambda i, j: jnp.where(i < nblocks, j, nsteps - 1)
    gain = pl.BlockSpec((1, d), lambda i, j: (0, 0))
    slab_spec = pl.BlockSpec((slab_rows, d), _lagged_slab_index(slabs))
    return pl.pallas_call(
        functools.partial(_outproj_resid_kernel, nblocks=nblocks, slabs=slabs, slab_rows=slab_rows),
        grid=(nblocks + 1, nsteps),
        in_specs=[pl.BlockSpec((tm, k1), block_row),
                  pl.BlockSpec((tm, k2), block_row),
                  pl.BlockSpec((k1, tn), lambda i, j: (0, w_col(i, j))),
                  pl.BlockSpec((k2, tn), lambda i, j: (1, w_col(i, j))),
                  slab_spec, gain, gain],
        out_specs=[slab_spec, slab_spec],
        out_shape=[jax.ShapeDtypeStruct((m, d), F32), jax.ShapeDtypeStruct((m, d), BF16)],
        scratch_shapes=[pltpu.VMEM((nsteps, tm, tn), F32), pltpu.VMEM((nsteps, tm, tn), F32)],
        compiler_params=pltpu.CompilerParams(dimension_semantics=("arbitrary", "arbitrary"),
                                             vmem_limit_bytes=TWO_ACC_VMEM_LIMIT_BYTES),
        name="outproj_resid",
    )(a1, a2, w, w, h, g_post, g_next)


def _qkv_kernel(cq_ref, ckv_ref, kr_ref, cos_ref, sin_ref, qg_ref, kvg_ref, wq_ref, wkv_ref,
                qn_ref, qr_ref, kn_ref, v_ref, kro_ref):
    cos = cos_ref[...]
    sin = sin_ref[...]
    tm = cos.shape[0]
    lane = lax.broadcasted_iota(jnp.int32, (tm, LANES), 1)
    first_half = (lane % QK_ROPE_DIM) < (QK_ROPE_DIM // 2)
    chunk = 512

    cqn = _rms(cq_ref[...], qg_ref[...]).astype(BF16)
    low_half = lane < QK_ROPE_DIM
    pair_width = 3 * LANES
    pairs_per_dot = 2
    for c in range(N_HEADS // (2 * pairs_per_dot)):
        cols = slice(c * pairs_per_dot * pair_width, (c + 1) * pairs_per_dot * pair_width)
        q = jnp.dot(cqn, wq_ref[:, cols], preferred_element_type=F32)
        for pp in range(pairs_per_dot):
            pair = c * pairs_per_dot + pp
            a, b, cc = (q[:, pp * pair_width + g * LANES:pp * pair_width + (g + 1) * LANES] for g in range(3))
            nope_odd = pltpu.roll(jnp.where(low_half, cc, b), QK_ROPE_DIM, 1)
            t = jnp.where(low_half, b, cc)
            swapped = jnp.where(first_half, pltpu.roll(t, LANES - 32, 1), pltpu.roll(t, 32, 1))
            qn_ref[:, 2 * pair * LANES:(2 * pair + 1) * LANES] = (a * Q_PRESCALE).astype(BF16)
            qn_ref[:, (2 * pair + 1) * LANES:(2 * pair + 2) * LANES] = (nope_odd * Q_PRESCALE).astype(BF16)
            qr_ref[:, pair * LANES:(pair + 1) * LANES] = ((t * cos + swapped * sin) * Q_PRESCALE).astype(BF16)

    ckvn = _rms(ckv_ref[...], kvg_ref[...]).astype(BF16)
    per_head = QK_NOPE_DIM + V_HEAD_DIM
    heads_per_chunk = chunk // per_head
    for c in range(N_HEADS // heads_per_chunk):
        kv = jnp.dot(ckvn, wkv_ref[:, c * chunk:(c + 1) * chunk], preferred_element_type=F32).astype(BF16)
        for hh in range(heads_per_chunk):
            head = c * heads_per_chunk + hh
            out = slice(head * QK_NOPE_DIM, (head + 1) * QK_NOPE_DIM)
            kn_ref[:, out] = kv[:, hh * per_head:hh * per_head + QK_NOPE_DIM]
            v_ref[:, out] = kv[:, hh * per_head + QK_NOPE_DIM:(hh + 1) * per_head]

    kr = kr_ref[...]
    kr_rot = kr * cos + pltpu.roll(kr, 32, 1) * sin
    zero = jnp.zeros_like(kr_rot)
    kro_ref[:, :LANES] = jnp.where(lane < QK_ROPE_DIM, kr_rot, zero).astype(BF16)
    kro_ref[:, LANES:] = jnp.where(lane >= QK_ROPE_DIM, kr_rot, zero).astype(BF16)


def qkv_project(z, cos, sin, q_gain, kv_gain, wq, wkv, layer, tm):
    n = z.shape[0]
    rope_w = N_HEADS * QK_ROPE_DIM
    row = lambda i: (i, 0)
    const = lambda i: (0, 0)
    weight = lambda i: (layer, 0, 0)
    return pl.pallas_call(
        _qkv_kernel,
        grid=(n // tm,),
        in_specs=[pl.BlockSpec((tm, Q_LORA_RANK), lambda i: (i, Z_Q // Q_LORA_RANK)),
                  pl.BlockSpec((tm, KV_LORA_RANK), lambda i: (i, Z_KV // KV_LORA_RANK)),
                  pl.BlockSpec((tm, LANES), lambda i: (i, Z_KR // LANES)),
                  pl.BlockSpec((tm, LANES), row),
                  pl.BlockSpec((tm, LANES), row),
                  pl.BlockSpec((1, Q_LORA_RANK), const),
                  pl.BlockSpec((1, KV_LORA_RANK), const),
                  pl.BlockSpec((None,) + wq.shape[1:], weight),
                  pl.BlockSpec((None,) + wkv.shape[1:], weight)],
        out_specs=[pl.BlockSpec((tm, MLA_WIDTH), row),
                   pl.BlockSpec((tm, rope_w), row),
                   pl.BlockSpec((tm, MLA_WIDTH), row),
                   pl.BlockSpec((tm, MLA_WIDTH), row),
                   pl.BlockSpec((tm, 2 * LANES), row)],
        out_shape=[jax.ShapeDtypeStruct((n, MLA_WIDTH), BF16),
                   jax.ShapeDtypeStruct((n, rope_w), BF16),
                   jax.ShapeDtypeStruct((n, MLA_WIDTH), BF16),
                   jax.ShapeDtypeStruct((n, MLA_WIDTH), BF16),
                   jax.ShapeDtypeStruct((n, 2 * LANES), BF16)],
        compiler_params=_params("parallel"),
        name="qkv_project",
    )(z, z, z, cos, sin, q_gain, kv_gain, wq, wkv)


def _scores(q, k):
    return lax.dot_general(q, k, (((1,), (1,)), ((), ())), preferred_element_type=F32)


def _online_softmax_update(s, v, state, mask):
    m_prev, l_prev, acc_prev = state
    if mask is not None:
        top = mask.shape[0]
        s_top = jnp.where(mask, s[:top], MASK_VALUE)
        s = s_top if s.shape[0] == top else jnp.concatenate([s_top, s[top:]], axis=0)
    m_next = jnp.maximum(m_prev, jnp.max(s, axis=1, keepdims=True))
    alpha = jnp.exp2(m_prev - m_next)
    ps = [jnp.exp2(s[:, c * LANES:(c + 1) * LANES] - m_next) for c in range(s.shape[1] // LANES)]
    l_next = alpha * l_prev + functools.reduce(lambda a, b: a + b, ps)
    p = jnp.concatenate(ps, axis=1).astype(BF16)
    acc_next = alpha * acc_prev + jnp.dot(p, v, preferred_element_type=F32)
    return m_next, l_next, acc_next


def _attn_kernel(qn_ref, qr_ref, kn_ref, kr_ref, v_ref, *rest, tq, tk, n_cast):
    w_refs = rest[:n_cast]
    o_ref = rest[n_cast]
    wo_refs = rest[n_cast + 1:2 * n_cast + 1]
    m_scr, l_scr, acc_scr = rest[2 * n_cast + 1:]
    for w_ref, wo_ref in zip(w_refs, wo_refs):
        wo_ref[...] = w_ref[...].astype(BF16)

    i = pl.program_id(2)
    per_tile = tq // tk
    group = min(per_tile, ATTN_CHUNKS_PER_ITERATION)
    m_scr[...] = jnp.full(m_scr.shape, MASK_VALUE, F32)
    l_scr[...] = jnp.zeros(l_scr.shape, F32)
    acc_scr[...] = jnp.zeros(acc_scr.shape, F32)

    def load_q(rows):
        return jnp.concatenate([qn_ref[rows, :], qr_ref[rows, :]], axis=1)

    def load_kv(start):
        k = jnp.concatenate([kn_ref[pl.ds(start, tk), :], kr_ref[pl.ds(start, tk), :]], axis=1)
        return k, v_ref[pl.ds(start, tk), :]

    def load_state(rows):
        return m_scr[rows, :], l_scr[rows, :], acc_scr[rows, :]

    def store_state(rows, state):
        m_scr[rows, :], l_scr[rows, :], acc_scr[rows, :] = state

    everything = slice(0, tq)

    def body(g, carry):
        q = load_q(everything)
        state = load_state(everything)
        for u in range(group):
            k, v = load_kv(pl.multiple_of((g * group + u) * tk, tk))
            state = _online_softmax_update(_scores(q, k), v, state, None)
        store_state(everything, state)
        return carry

    lax.fori_loop(0, i * (per_tile // group), body, 0)

    for d in range(per_tile):
        rows = slice(d * tk, tq)
        k, v = load_kv(pl.multiple_of(i * tq + d * tk, tk))
        row = lax.broadcasted_iota(jnp.int32, (tk, tk), 0)
        col = lax.broadcasted_iota(jnp.int32, (tk, tk), 1)
        state = _online_softmax_update(_scores(load_q(rows), k), v, load_state(rows), col <= row)
        store_state(rows, state)

    l = jnp.sum(l_scr[...], axis=1, keepdims=True)
    o_ref[...] = (acc_scr[...] / l).astype(o_ref.dtype)


def _slab_rows(total_rows, steps):
    rows = BF16_SUBLANES
    while total_rows % rows or total_rows // rows > steps:
        rows += BF16_SUBLANES
    return rows


def causal_attention(qn, qr, kn, kr, v, cast_weights, layer, batch, seq, tq, tk):
    n = qn.shape[0]
    nq = seq // tq
    steps = batch * N_HEADS * nq

    w_in_specs, w_out_specs, w_out_shapes = [], [], []
    for w in cast_weights:
        _, nrows, ncols = w.shape
        rows = _slab_rows(nrows, steps)
        last = nrows // rows - 1
        slab = lambda b, h, i, last=last: (jnp.minimum((b * N_HEADS + h) * nq + i, last), 0)
        w_in_specs.append(pl.BlockSpec((None, rows, ncols), lambda b, h, i, slab=slab: (layer,) + slab(b, h, i)))
        w_out_specs.append(pl.BlockSpec((rows, ncols), slab))
        w_out_shapes.append(jax.ShapeDtypeStruct((nrows, ncols), BF16))

    return pl.pallas_call(
        functools.partial(_attn_kernel, tq=tq, tk=tk, n_cast=len(cast_weights)),
        grid=(batch, N_HEADS, nq),
        in_specs=[pl.BlockSpec((tq, QK_NOPE_DIM), lambda b, h, i: (b * nq + i, h)),
                  pl.BlockSpec((tq, ROPE_PAIR_WIDTH), lambda b, h, i: (b * nq + i, h // 2)),
                  pl.BlockSpec((seq, QK_NOPE_DIM), lambda b, h, i: (b, h)),
                  pl.BlockSpec((seq, ROPE_PAIR_WIDTH), lambda b, h, i: (b, h % 2)),
                  pl.BlockSpec((seq, V_HEAD_DIM), lambda b, h, i: (b, h))] + w_in_specs,
        out_specs=[pl.BlockSpec((tq, V_HEAD_DIM), lambda b, h, i: (b * nq + i, h))] + w_out_specs,
        out_shape=[jax.ShapeDtypeStruct((n, MLA_WIDTH), BF16)] + w_out_shapes,
        scratch_shapes=[pltpu.VMEM((tq, LANES), F32),
                        pltpu.VMEM((tq, LANES), F32),
                        pltpu.VMEM((tq, V_HEAD_DIM), F32)],
        compiler_params=_params("arbitrary", "arbitrary", "arbitrary"),
        name="causal_attention",
    )(qn, qr, kn, kr, v, *cast_weights)


def _pool_kernel(u_ref, halo_ref, w_ref, scale_ref, o_ref, *, ts):
    i = pl.program_id(1)
    t = i * ts + lax.broadcasted_iota(jnp.int32, (ts, 1), 0)
    for g, window in enumerate(POOL_WINDOWS):
        sl = slice(g * POOL_GROUP_DIM, (g + 1) * POOL_GROUP_DIM)
        x = u_ref[:, sl]
        halo = jnp.where(i > 0, halo_ref[:, sl], 0.0)
        s = jnp.concatenate([halo, x], axis=0)
        shift = 1
        while shift < window:
            s = s + pltpu.roll(s, shift, 0)
            shift *= 2
        cnt = jnp.minimum(t + 1, window).astype(F32)
        d = s[POOL_HALO:, :] / cnt - x
        y = jnp.dot(d.astype(BF16), w_ref[g], preferred_element_type=F32)
        o_ref[:, sl] = (y * scale_ref[:, sl]).astype(o_ref.dtype)


def multiscale_pool(z, pool_w, pool_scale, layer, batch, seq, ts):
    n = z.shape[0]
    ns = seq // ts
    halo_blocks = ts // POOL_HALO
    return pl.pallas_call(
        functools.partial(_pool_kernel, ts=ts),
        grid=(batch, ns),
        in_specs=[pl.BlockSpec((ts, POOL_WIDTH), lambda b, i: (b * ns + i, Z_U // POOL_WIDTH)),
                  pl.BlockSpec((POOL_HALO, POOL_WIDTH),
                               lambda b, i: (jnp.maximum((b * ns + i) * halo_blocks - 1, 0), 0)),
                  pl.BlockSpec((None,) + pool_w.shape[1:], lambda b, i: (layer, 0, 0, 0)),
                  pl.BlockSpec((1, POOL_WIDTH), lambda b, i: (0, 0))],
        out_specs=pl.BlockSpec((ts, POOL_WIDTH), lambda b, i: (b * ns + i, 0)),
        out_shape=jax.ShapeDtypeStruct((n, POOL_WIDTH), BF16),
        compiler_params=_params("parallel", "parallel"),
        name="multiscale_pool",
    )(z, z, pool_w, pool_scale)


def _ffn_resid_kernel(a_ref, wg_ref, wu_ref, wd_ref, h_ref, gpost_ref, *rest,
                      chunk, nblocks, slabs, slab_rows, with_next):
    if with_next:
        gnext_ref, ho_ref, ao_ref, acc0, acc1 = rest
    else:
        ho_ref, acc0, acc1 = rest
        gnext_ref = ao_ref = None
    i = pl.program_id(0)
    f = pl.program_id(1)

    @pl.when(jnp.logical_and(i == 0, f == 0))
    def _():
        acc1[...] = jnp.zeros(acc1.shape, F32)

    def prepare(acc):
        @pl.when(f == 0)
        def _():
            acc[...] = jnp.zeros(acc.shape, F32)

    def matmuls(acc):
        a = a_ref[...]
        g = jnp.dot(a, wg_ref[...], preferred_element_type=F32)
        u = jnp.dot(a, wu_ref[...], preferred_element_type=F32)
        p = (g * (1.0 / (1.0 + jnp.exp(-g))) * u).astype(BF16)
        for c in range(acc.shape[1] // chunk):
            sl = slice(c * chunk, (c + 1) * chunk)
            acc[:, sl] += jnp.dot(p, wd_ref[:, sl], preferred_element_type=F32)

    def epilogue(acc):
        _resid_epilogue(lambda rows: acc[rows, :], f, slabs, slab_rows,
                        h_ref, gpost_ref, gnext_ref, ho_ref, ao_ref)

    _two_stream_step(i, nblocks, (acc0, acc1), matmuls, epilogue, prepare)


def swiglu_resid(a, w_gate, w_up, w_down, h, g_post, g_next, tm, tf):
    n, d = a.shape
    d_ff = w_gate.shape[1]
    nblocks = n // tm
    nf = d_ff // tf
    slab_rows = 2 * BF16_SUBLANES
    slabs = tm // slab_rows
    assert slabs <= nf and tm % slab_rows == 0
    with_next = g_next is not None

    def f_index(i, f):
        return jnp.where(i < nblocks, f, nf - 1)

    gain = pl.BlockSpec((1, d), lambda i, f: (0, 0))
    slab_spec = pl.BlockSpec((slab_rows, d), _lagged_slab_index(slabs))
    in_specs = [pl.BlockSpec((tm, d), lambda i, f: (jnp.minimum(i, nblocks - 1), 0),
                             pipeline_mode=pl.Buffered(1)),
                pl.BlockSpec((d, tf), lambda i, f: (0, f_index(i, f))),
                pl.BlockSpec((d, tf), lambda i, f: (0, f_index(i, f))),
                pl.BlockSpec((tf, d), lambda i, f: (f_index(i, f), 0)),
                slab_spec, gain]
    args = [a, w_gate, w_up, w_down, h, g_post]
    out_specs = [slab_spec]
    out_shape = [jax.ShapeDtypeStruct((n, d), F32)]
    if with_next:
        in_specs.append(gain)
        args.append(g_next)
        out_specs.append(slab_spec)
        out_shape.append(jax.ShapeDtypeStruct((n, d), BF16))
    outs = pl.pallas_call(
        functools.partial(_ffn_resid_kernel, chunk=512, nblocks=nblocks, slabs=slabs,
                          slab_rows=slab_rows, with_next=with_next),
        grid=(nblocks + 1, nf),
        in_specs=in_specs,
        out_specs=out_specs,
        out_shape=out_shape,
        scratch_shapes=[pltpu.VMEM((tm, d), F32), pltpu.VMEM((tm, d), F32)],
        compiler_params=pltpu.CompilerParams(dimension_semantics=("arbitrary", "arbitrary"),
                                             vmem_limit_bytes=TWO_ACC_VMEM_LIMIT_BYTES),
        name="swiglu_resid",
    )(*args)
    return (outs[0], outs[1]) if with_next else (outs[0], None)


def _prep_w_in(w_in):
    o1 = Q_LORA_RANK
    o2 = o1 + KV_LORA_RANK
    o3 = o2 + QK_ROPE_DIM
    w = w_in.astype(BF16)
    kr = w[..., o2:o3]
    return jnp.concatenate([w[..., o3:], w[..., o1:o2], kr, kr, w[..., :o1]], axis=-1)


def kernel(x, positions, w_in, q_norm, w_q_up, kv_norm, w_kv_up, pool_w, pool_scale, w_out,
           norm_pre_mix, norm_post_mix, norm_pre_ffn, norm_post_ffn, w_gate, w_up, w_down):
    batch, seq, d_model = x.shape
    depth = w_in.shape[0]
    n = batch * seq

    tm_big = min(1024, n)
    tm_mid = min(512, n)
    rows_rope = min(2048, n)
    tq = min(2048, seq)
    tk = min(256, seq)
    ts = min(512, seq)
    assert all(n % t == 0 for t in (tm_big, tm_mid, rows_rope)), (batch, seq)
    assert seq % tq == 0 and tq % tk == 0 and seq % ts == 0 and ts % POOL_HALO == 0, seq

    inv_freq = ROPE_THETA ** (-jnp.arange(0, QK_ROPE_DIM, 2, dtype=F32) / QK_ROPE_DIM)
    invf = jnp.tile(inv_freq, LANES // inv_freq.shape[0])[None, :]
    half = QK_ROPE_DIM // 2
    sign = jnp.where((jnp.arange(LANES) % QK_ROPE_DIM) < half, -1.0, 1.0).astype(F32)[None, :]
    cos, sin = rope_tables(positions.reshape(n, 1), invf, sign, rows_rope)

    gain = lambda g, l: g[l][None, :]
    h = x.reshape(n, d_model)
    a = None
    w_in_b = _prep_w_in(w_in)
    w_q_b = w_q_up.astype(BF16)
    w_kv_b = w_kv_up.astype(BF16)
    pool_w_b = pool_w.astype(BF16)
    for l in range(depth):
        if a is None:
            z = norm_matmul(h, gain(norm_pre_mix, l), w_in_b, l, tm_big, 512)
        else:
            z = matmul(a, w_in_b, l, F32, tm_big, 512)
        qn, qr, kn, v, kr = qkv_project(z, cos, sin, gain(q_norm, l), gain(kv_norm, l),
                                        w_q_b, w_kv_b, l, tm_mid)
        attn, w_out_b, w_gate_b, w_up_b, w_down_b = causal_attention(
            qn, qr, kn, kr, v, (w_out, w_gate, w_up, w_down), l, batch, seq, tq, tk)
        pool = multiscale_pool(z, pool_w_b, gain(pool_scale, l), l, batch, seq, ts)
        h, a = outproj_resid(attn, pool, w_out_b, h, gain(norm_post_mix, l), gain(norm_pre_ffn, l),
                             tm_big, 256)
        g_next = gain(norm_pre_mix, l + 1) if l + 1 < depth else None
        h, a = swiglu_resid(a, w_gate_b, w_up_b, w_down_b, h, gain(norm_post_ffn, l), g_next, tm_big, 256)
    return h.reshape(batch, seq, d_model)
```
